```python
import math
import jax
import jax.numpy as jnp
from jax import lax
import numpy as np

D_MODEL = 1024
BATCH = 1
SEQ = 16384
DEPTH = 2
DEC_BATCH = 8
DEC_SEQ = 4096
PAST_LEN = 128

N_MEM = 256
N_MIXERS = 2
N_GLA_LAYERS = (DEPTH + N_MIXERS - 1) // N_MIXERS
N_GDN_LAYERS = DEPTH // N_MIXERS
CHUNK = 64

GLA_HEADS = 4
GLA_DK = D_MODEL // 2 // GLA_HEADS
GLA_DV = D_MODEL // GLA_HEADS
GLA_RANK = 16
GLA_GATE_NORMALIZER = 16.0

GDN_QK_HEADS = D_MODEL // 128
GDN_V_HEADS = 2 * GDN_QK_HEADS
GDN_DK = 128
GDN_DV = 128
GDN_CONV = 4
GDN_CONV_PAD = (2, 1)

XA_HEADS = 4
XA_DH = D_MODEL // XA_HEADS

D_FF = 2816

ALPHA = (2.0 * DEPTH) ** 0.25
BETA_INIT = (8.0 * DEPTH) ** -0.25
LN_EPS = 1e-5
NORM_EPS = 1e-6

kernel_name = 'hybrid_gla_gdn_macaron_encoder'


def layer_norm(x, g, b):
    xf = x.astype(jnp.float32)
    mu = jnp.mean(xf, axis=-1, keepdims=True)
    var = jnp.mean(jnp.square(xf - mu), axis=-1, keepdims=True)
    return ((xf - mu) * lax.rsqrt(var + LN_EPS) * g.astype(jnp.float32) + b.astype(jnp.float32)).astype(x.dtype)


def rms_norm(x, w):
    xf = x.astype(jnp.float32)
    return xf * lax.rsqrt(jnp.mean(jnp.square(xf), axis=-1, keepdims=True) + NORM_EPS) * w.astype(jnp.float32)


def l2_normalize(x):
    return x * lax.rsqrt(jnp.sum(jnp.square(x), axis=-1, keepdims=True) + NORM_EPS)


def flip_seq(t, reverse):
    return jnp.flip(t, axis=1) if reverse else t


def to_chunks(t):
    b, s = t.shape[:2]
    t = t.reshape((b, s // CHUNK, CHUNK) + t.shape[2:])
    return jnp.moveaxis(t, 3, 1)


def from_chunks(t):
    t = jnp.moveaxis(t, 1, 3)
    return t.reshape((t.shape[0], t.shape[1] * t.shape[2]) + t.shape[3:])


def scan_chunks(step, state0, xs):
    xs = tuple(jnp.moveaxis(a, 2, 0) for a in xs)
    _, out = lax.scan(step, state0, xs)
    return jnp.moveaxis(out, 0, 2)


def swiglu_ffn(x, w_in, w_out):
    gate, up = jnp.split(jnp.matmul(x, w_in), 2, axis=-1)
    return jnp.matmul(jax.nn.silu(gate) * up, w_out)


def gla_chunked(q, k, v, g):
    c = q.shape[3]
    b = jnp.cumsum(g, axis=3)
    b_last = b[:, :, :, -1:, :]
    b_mid = b[:, :, :, c // 2 - 1:c // 2, :]
    causal = jnp.tril(jnp.ones((c, c), dtype=bool))
    scores = jnp.einsum('bhncd,bhnsd->bhncs', q * jnp.exp(b - b_mid), k * jnp.exp(b_mid - b))
    o_intra = jnp.einsum('bhncs,bhnsv->bhncv', jnp.where(causal, scores, 0.0), v)
    q_start = q * jnp.exp(b)
    k_end = k * jnp.exp(b_last - b)
    chunk_decay = jnp.exp(b_last[:, :, :, 0, :])

    def step(state, xs):
        q_c, k_c, v_c, d_c = xs
        o = jnp.einsum('bhcd,bhdv->bhcv', q_c, state)
        state = state * d_c[..., None] + jnp.einsum('bhcd,bhcv->bhdv', k_c, v_c)
        return state, o

    bsz, h, _, _, dk = q.shape
    state0 = jnp.zeros((bsz, h, dk, v.shape[-1]), jnp.float32)
    o_inter = scan_chunks(step, state0, (q_start, k_end, v, chunk_decay))
    return o_intra + o_inter


def gla_mixer(x, w_in, w_gate_down, w_gate_up, b_gate, norm_w, w_out):
    bsz, s, _ = x.shape
    h, dk, dv = GLA_HEADS, GLA_DK, GLA_DV
    proj = jnp.matmul(x, w_in).astype(jnp.float32)
    q, k, v, r = jnp.split(proj, [h * dk, 2 * h * dk, 2 * h * dk + h * dv], axis=-1)
    q = q.reshape(bsz, s, h, dk) * (dk ** -0.5)
    k = k.reshape(bsz, s, h, dk)
    v = v.reshape(bsz, s, h, dv)

    def direction(d):
        rev = d == 1
        logit = jnp.matmul(jnp.matmul(x, w_gate_down[d]), w_gate_up[d]).astype(jnp.float32) + b_gate[d].astype(jnp.float32)
        g = (jax.nn.log_sigmoid(logit) / GLA_GATE_NORMALIZER).reshape(bsz, s, h, dk)
        od = gla_chunked(to_chunks(flip_seq(q, rev)), to_chunks(flip_seq(k, rev)),
                         to_chunks(flip_seq(v, rev)), to_chunks(flip_seq(g, rev)))
        return flip_seq(from_chunks(od), rev)

    o = direction(0) + direction(1)
    o = rms_norm(o, norm_w) * jax.nn.silu(r.reshape(bsz, s, h, dv))
    return jnp.matmul(o.reshape(bsz, s, h * dv).astype(x.dtype), w_out)


def gated_delta_chunked(q, k, v, beta, g):
    c = q.shape[3]
    gc = jnp.cumsum(g, axis=3)
    gc_last = gc[..., -1]
    incl = jnp.tril(jnp.ones((c, c), dtype=bool))
    strict = jnp.tril(jnp.ones((c, c), dtype=bool), -1)
    decay = jnp.exp(jnp.where(incl, gc[..., :, None] - gc[..., None, :], -jnp.inf))
    k_beta = k * beta[..., None]
    a = jnp.where(strict, jnp.einsum('bhncd,bhnsd->bhncs', k_beta, k) * decay, 0.0)
    eye = jnp.eye(c, dtype=a.dtype)
    t_mat = lax.linalg.triangular_solve(a + eye, jnp.broadcast_to(eye, a.shape),
                                        left_side=True, lower=True, unit_diagonal=True)
    u = jnp.einsum('bhncs,bhnsv->bhncv', t_mat, v * beta[..., None])
    w = jnp.einsum('bhncs,bhnsd->bhncd', t_mat, k_beta * jnp.exp(gc)[..., None])
    attn = jnp.einsum('bhncd,bhnsd->bhncs', q, k) * decay
    q_start = q * jnp.exp(gc)[..., None]
    k_end = k * jnp.exp(gc_last[..., None] - gc)[..., None]
    chunk_decay = jnp.exp(gc_last)

    def step(state, xs):
        u_c, w_c, attn_c, q_c, k_c, d_c = xs
        v_new = u_c - jnp.einsum('bhcd,bhdv->bhcv', w_c, state)
        o = jnp.einsum('bhcd,bhdv->bhcv', q_c, state) + jnp.einsum('bhcs,bhsv->bhcv', attn_c, v_new)
        state = state * d_c[..., None, None] + jnp.einsum('bhcd,bhcv->bhdv', k_c, v_new)
        return state, o

    bsz, h, _, _, dk = q.shape
    state0 = jnp.zeros((bsz, h, dk, v.shape[-1]), jnp.float32)
    return scan_chunks(step, state0, (u, w, attn, q_start, k_end, chunk_decay))


def centred_depthwise_conv(x, w):
    ch = x.shape[-1]
    return lax.conv_general_dilated(x, w.reshape(GDN_CONV, 1, ch).astype(x.dtype), window_strides=(1,),
                                    padding=[GDN_CONV_PAD], dimension_numbers=('NWC', 'WIO', 'NWC'),
                                    feature_group_count=ch)


def gdn_mixer(x, w_in, conv_w, w_ab, a_log, dt_bias, norm_w, w_out):
    bsz, s, _ = x.shape
    hk, hv, dk, dv = GDN_QK_HEADS, GDN_V_HEADS, GDN_DK, GDN_DV
    n_qkv = 2 * hk * dk + hv * dv
    proj = jnp.matmul(x, w_in).astype(jnp.float32)
    qkv = jax.nn.silu(centred_depthwise_conv(proj[..., :n_qkv], conv_w))
    z = proj[..., n_qkv:]
    q, k, v = jnp.split(qkv, [hk * dk, 2 * hk * dk], axis=-1)
    rep = hv // hk
    q = jnp.repeat(l2_normalize(q.reshape(bsz, s, hk, dk)), rep, axis=2) * (dk ** -0.5)
    k = jnp.repeat(l2_normalize(k.reshape(bsz, s, hk, dk)), rep, axis=2)
    v = v.reshape(bsz, s, hv, dv)

    def direction(d):
        rev = d == 1
        a, bt = jnp.split(jnp.matmul(x, w_ab[d]).astype(jnp.float32), 2, axis=-1)
        g = -jnp.exp(a_log[d].astype(jnp.float32)) * jax.nn.softplus(a + dt_bias[d].astype(jnp.float32))
        beta = jax.nn.sigmoid(bt)
        od = gated_delta_chunked(to_chunks(flip_seq(q, rev)), to_chunks(flip_seq(k, rev)),
                                 to_chunks(flip_seq(v, rev)), to_chunks(flip_seq(beta, rev)),
                                 to_chunks(flip_seq(g, rev)))
        return flip_seq(from_chunks(od), rev)

    o = direction(0) + direction(1)
    o = rms_norm(o, norm_w) * jax.nn.silu(z.reshape(bsz, s, hv, dv))
    return jnp.matmul(o.reshape(bsz, s, hv * dv).astype(x.dtype), w_out)


def cross_attention(x, mem, w_q, w_kv, w_o):
    bsz, s, _ = x.shape
    q = jnp.matmul(x, w_q).reshape(bsz, s, XA_HEADS, XA_DH)
    k, v = jnp.split(jnp.matmul(mem, w_kv), 2, axis=-1)
    k = k.reshape(bsz, N_MEM, XA_HEADS, XA_DH)
    v = v.reshape(bsz, N_MEM, XA_HEADS, XA_DH)
    scores = jnp.einsum('bthd,bmhd->bhtm', q, k).astype(jnp.float32) * (XA_DH ** -0.5)
    p = jax.nn.softmax(scores, axis=-1).astype(v.dtype)
    o = jnp.einsum('bhtm,bmhd->bthd', p, v).reshape(bsz, s, D_MODEL)
    return jnp.matmul(o, w_o)


def encoder_trunk(x, mem, ffn_w_in, ffn_w_out, ln_g, ln_b,
                  gla_w_in, gla_w_gate_down, gla_w_gate_up, gla_b_gate, gla_norm_w, gla_w_out,
                  gdn_w_in, gdn_conv_w, gdn_w_ab, gdn_a_log, gdn_dt_bias, gdn_norm_w, gdn_w_out,
                  xa_w_q, xa_w_kv, xa_w_o):
    for i in range(DEPTH):
        x = layer_norm(ALPHA * x + 0.5 * swiglu_ffn(x, ffn_w_in[i, 0], ffn_w_out[i, 0]), ln_g[i, 0], ln_b[i, 0])
        j = i // N_MIXERS
        if i % N_MIXERS == 0:
            h = gla_mixer(x, gla_w_in[j], gla_w_gate_down[j], gla_w_gate_up[j], gla_b_gate[j],
                          gla_norm_w[j], gla_w_out[j])
        else:
            h = gdn_mixer(x, gdn_w_in[j], gdn_conv_w[j], gdn_w_ab[j], gdn_a_log[j], gdn_dt_bias[j],
                          gdn_norm_w[j], gdn_w_out[j])
        x = layer_norm(ALPHA * x + h, ln_g[i, 1], ln_b[i, 1])
        x = layer_norm(ALPHA * x + cross_attention(x, mem, xa_w_q[i], xa_w_kv[i], xa_w_o[i]), ln_g[i, 2], ln_b[i, 2])
        x = layer_norm(ALPHA * x + 0.5 * swiglu_ffn(x, ffn_w_in[i, 1], ffn_w_out[i, 1]), ln_g[i, 3], ln_b[i, 3])
    return x


def setup_inputs(seed: int = 0) -> dict:
    key = jax.random.key(seed)
    ks = jax.random.split(key, 24)

    def nrm(k, shape, scale):
        return jax.random.normal(k, shape, jnp.float32) * scale

    d = D_MODEL
    gla_cols = 2 * GLA_HEADS * GLA_DK + 2 * GLA_HEADS * GLA_DV
    gdn_qkv = 2 * GDN_QK_HEADS * GDN_DK + GDN_V_HEADS * GDN_DV
    gdn_cols = gdn_qkv + GDN_V_HEADS * GDN_DV
    dt = jnp.exp(jax.random.uniform(ks[18], (N_GDN_LAYERS, 2, GDN_V_HEADS), jnp.float32,
                                    math.log(1e-3), math.log(1e-1)))
    return {
        'x_prompt': nrm(ks[0], (BATCH, SEQ, d), 1.0),
        'x_sample': nrm(ks[1], (DEC_BATCH, DEC_SEQ, d), 1.0),
        'mem_prompt': nrm(ks[2], (BATCH, N_MEM, d), 1.0),
        'mem_sample': nrm(ks[3], (DEC_BATCH, N_MEM, d), 1.0),
        'ffn_w_in': nrm(ks[4], (DEPTH, 2, d, 2 * D_FF), d ** -0.5),
        'ffn_w_out': nrm(ks[5], (DEPTH, 2, D_FF, d), BETA_INIT * D_FF ** -0.5),
        'ln_g': 1.0 + nrm(ks[6], (DEPTH, 4, d), 0.02),
        'ln_b': nrm(ks[7], (DEPTH, 4, d), 0.02),
        'gla_w_in': nrm(ks[8], (N_GLA_LAYERS, d, gla_cols), d ** -0.5),
        'gla_w_gate_down': nrm(ks[9], (N_GLA_LAYERS, 2, d, GLA_RANK), d ** -0.5),
        'gla_w_gate_up': nrm(ks[10], (N_GLA_LAYERS, 2, GLA_RANK, GLA_HEADS * GLA_DK), GLA_RANK ** -0.5),
        'gla_b_gate': nrm(ks[11], (N_GLA_LAYERS, 2, GLA_HEADS * GLA_DK), 0.1),
        'gla_norm_w': 1.0 + nrm(ks[12], (N_GLA_LAYERS, GLA_DV), 0.02),
        'gla_w_out': nrm(ks[13], (N_GLA_LAYERS, GLA_HEADS * GLA_DV, d), BETA_INIT * (GLA_HEADS * GLA_DV) ** -0.5),
        'gdn_w_in': nrm(ks[14], (N_GDN_LAYERS, d, gdn_cols), d ** -0.5),
        'gdn_conv_w': nrm(ks[15], (N_GDN_LAYERS, GDN_CONV, gdn_qkv), GDN_CONV ** -0.5),
        'gdn_w_ab': nrm(ks[16], (N_GDN_LAYERS, 2, d, 2 * GDN_V_HEADS), d ** -0.5),
        'gdn_a_log': jnp.log(jax.random.uniform(ks[17], (N_GDN_LAYERS, 2, GDN_V_HEADS), jnp.float32, 1.0, 16.0)),
        'gdn_dt_bias': dt + jnp.log(-jnp.expm1(-dt)),
        'gdn_norm_w': 1.0 + nrm(ks[19], (N_GDN_LAYERS, GDN_DV), 0.02),
        'gdn_w_out': nrm(ks[20], (N_GDN_LAYERS, GDN_V_HEADS * GDN_DV, d), BETA_INIT * (GDN_V_HEADS * GDN_DV) ** -0.5),
        'xa_w_q': nrm(ks[21], (DEPTH, d, d), d ** -0.5),
        'xa_w_kv': nrm(ks[22], (DEPTH, d, 2 * d), d ** -0.5),
        'xa_w_o': nrm(ks[23], (DEPTH, d, d), BETA_INIT * d ** -0.5),
    }


def reference(x_prompt, x_sample, mem_prompt, mem_sample, ffn_w_in, ffn_w_out, ln_g, ln_b,
              gla_w_in, gla_w_gate_down, gla_w_gate_up, gla_b_gate, gla_norm_w, gla_w_out,
              gdn_w_in, gdn_conv_w, gdn_w_ab, gdn_a_log, gdn_dt_bias, gdn_norm_w, gdn_w_out,
              xa_w_q, xa_w_kv, xa_w_o):
    weights = (ffn_w_in, ffn_w_out, ln_g, ln_b,
               gla_w_in, gla_w_gate_down, gla_w_gate_up, gla_b_gate, gla_norm_w, gla_w_out,
               gdn_w_in, gdn_conv_w, gdn_w_ab, gdn_a_log, gdn_dt_bias, gdn_norm_w, gdn_w_out,
               xa_w_q, xa_w_kv, xa_w_o)
    y_prompt = encoder_trunk(x_prompt, mem_prompt, *weights)
    y_sample = encoder_trunk(x_sample, mem_sample, *weights)
    return (y_prompt, y_sample)
```

```python
import functools

import jax
import jax.numpy as jnp
from jax import lax
from jax.experimental import pallas as pl
from jax.experimental.pallas import tpu as pltpu

F32 = jnp.float32
BF16 = jnp.bfloat16

DEPTH = 2
N_MIXERS = 2
CHUNK = 64
GLA_HEADS = 4
GLA_GATE_NORMALIZER = 16.0
GDN_QK_HEADS = 8
GDN_V_HEADS = 16
GDN_DK = 128
GDN_CONV = 4
XA_HEADS = 4
ALPHA = (2.0 * DEPTH) ** 0.25
LN_EPS = 1e-5
NORM_EPS = 1e-6

VMEM_LIMIT_BYTES = 56 * 1024 * 1024
HALO = 8


def _cparams(*sem):
    return pltpu.CompilerParams(dimension_semantics=sem, vmem_limit_bytes=VMEM_LIMIT_BYTES)


def _resident(shape):
    nd = len(shape)
    return pl.BlockSpec(shape, lambda *_: (0,) * nd, pipeline_mode=pl.Buffered(1))


def _dot(a, b):
    return jnp.dot(a, b, preferred_element_type=F32)


def _dot_nt(a, b):
    return lax.dot_general(a, b, (((1,), (1,)), ((), ())), preferred_element_type=F32)


def _dot_tn(a, b):
    return lax.dot_general(a, b, (((0,), (0,)), ((), ())), preferred_element_type=F32)


def _bmm(a, b):
    return jnp.einsum('nmk,nkp->nmp', a, b, preferred_element_type=F32)


def _bmm_nt(a, b):
    return jnp.einsum('nmk,npk->nmp', a, b, preferred_element_type=F32)


def _bmm_tn(a, b):
    return jnp.einsum('nkm,nkp->nmp', a, b, preferred_element_type=F32)


def _sigmoid(x):
    return jax.nn.sigmoid(x)


def _silu(x):
    return x * _sigmoid(x)


def _softplus(x):
    return jnp.maximum(x, 0.0) + jnp.log1p(jnp.exp(-jnp.abs(x)))


def _layer_norm(y, g, b):
    mu = jnp.mean(y, axis=-1, keepdims=True)
    yc = y - mu
    var = jnp.mean(yc * yc, axis=-1, keepdims=True)
    return yc * lax.rsqrt(var + LN_EPS) * g + b


def _chunk_cumsum(x, axis, reverse):
    n = x.shape[axis]
    pos = lax.broadcasted_iota(jnp.int32, x.shape, axis) % CHUNK
    s = 1
    while s < CHUNK:
        if reverse:
            shifted = pltpu.roll(x, n - s, axis)
            keep = pos < CHUNK - s
        else:
            shifted = pltpu.roll(x, s, axis)
            keep = pos >= s
        x = x + jnp.where(keep, shifted, 0.0)
        s *= 2
    return x


def _ffn_ln_kernel(x_ref, win_ref, wout_ref, g_ref, b_ref, o_ref, *, d_ff, n_split):
    x = x_ref[...]
    xb = x.astype(BF16)
    fc = d_ff // n_split
    acc = jnp.zeros(x.shape, F32)
    for c in range(n_split):
        lo = c * fc
        gate = _dot(xb, win_ref[:, lo:lo + fc])
        up = _dot(xb, win_ref[:, d_ff + lo:d_ff + lo + fc])
        h = (_silu(gate) * up).astype(BF16)
        acc = acc + _dot(h, wout_ref[lo:lo + fc, :])
    y = ALPHA * x + 0.5 * acc
    o_ref[...] = _layer_norm(y, g_ref[...], b_ref[...])


def _ffn_ln(x, w_in, w_out, g, b, *, tm):
    t, d = x.shape
    d_ff = w_out.shape[0]
    tm = min(tm, t)
    return pl.pallas_call(
        functools.partial(_ffn_ln_kernel, d_ff=d_ff, n_split=2),
        grid=(t // tm,),
        in_specs=[
            pl.BlockSpec((tm, d), lambda i: (i, 0)),
            _resident(w_in.shape),
            _resident(w_out.shape),
            _resident(g.shape),
            _resident(b.shape),
        ],
        out_specs=pl.BlockSpec((tm, d), lambda i: (i, 0)),
        out_shape=jax.ShapeDtypeStruct((t, d), F32),
        compiler_params=_cparams("parallel"),
        name="ffn_ln",
    )(x, w_in, w_out, g, b)


def _proj_kernel(x_ref, w_ref, o_ref):
    o_ref[...] = _dot(x_ref[...].astype(BF16), w_ref[...]).astype(o_ref.dtype)


def _proj(x, w, *, tm, out_dtype):
    t, d = x.shape
    n = w.shape[1]
    tm = min(tm, t)
    return pl.pallas_call(
        _proj_kernel,
        grid=(t // tm,),
        in_specs=[pl.BlockSpec((tm, d), lambda i: (i, 0)), _resident(w.shape)],
        out_specs=pl.BlockSpec((tm, n), lambda i: (i, 0)),
        out_shape=jax.ShapeDtypeStruct((t, n), out_dtype),
        compiler_params=_cparams("parallel"),
        name="mem_kv_proj",
    )(x, w)


def _xattn_ln_kernel(x_ref, kv_ref, wq_ref, wo_ref, g_ref, b_ref, o_ref, *, heads):
    x = x_ref[0]
    d = x.shape[-1]
    dh = d // heads
    q = _dot(x.astype(BF16), wq_ref[...]).astype(BF16)
    outs = []
    for h in range(heads):
        qh = q[:, h * dh:(h + 1) * dh]
        kh = kv_ref[0, :, h * dh:(h + 1) * dh]
        vh = kv_ref[0, :, d + h * dh:d + (h + 1) * dh]
        s = _dot_nt(qh, kh) * (dh ** -0.5)
        e = jnp.exp(s - jnp.max(s, axis=-1, keepdims=True))
        den = jnp.sum(e, axis=-1, keepdims=True)
        outs.append(_dot(e.astype(BF16), vh) / den)
    o = jnp.concatenate(outs, axis=-1).astype(BF16)
    y = ALPHA * x + _dot(o, wo_ref[...])
    o_ref[0] = _layer_norm(y, g_ref[...], b_ref[...])


def _xattn_ln(x, kv, w_q, w_o, g, b, *, tm):
    bsz, s, d = x.shape
    n_mem = kv.shape[1]
    tm = min(tm, s)
    return pl.pallas_call(
        functools.partial(_xattn_ln_kernel, heads=XA_HEADS),
        grid=(bsz, s // tm),
        in_specs=[
            pl.BlockSpec((1, tm, d), lambda bi, i: (bi, i, 0)),
            pl.BlockSpec((1, n_mem, 2 * d), lambda bi, i: (bi, 0, 0)),
            _resident(w_q.shape),
            _resident(w_o.shape),
            _resident(g.shape),
            _resident(b.shape),
        ],
        out_specs=pl.BlockSpec((1, tm, d), lambda bi, i: (bi, i, 0)),
        out_shape=jax.ShapeDtypeStruct((bsz, s, d), F32),
        compiler_params=_cparams("parallel", "parallel"),
        name="xattn_ln",
    )(x, kv, w_q, w_o, g, b)


def _mixer_out_kernel(of_ref, or_ref, r_ref, x_ref, nw_ref, wo_ref, g_ref, b_ref, o_ref, *, dv):
    o = of_ref[...] + or_ref[...]
    nw = nw_ref[...]
    parts = []
    for h in range(o.shape[-1] // dv):
        oh = o[:, h * dv:(h + 1) * dv]
        ms = jnp.mean(oh * oh, axis=-1, keepdims=True)
        parts.append(oh * lax.rsqrt(ms + NORM_EPS) * nw)
    gated = (jnp.concatenate(parts, axis=-1) * _silu(r_ref[...])).astype(BF16)
    y = ALPHA * x_ref[...] + _dot(gated, wo_ref[...])
    o_ref[...] = _layer_norm(y, g_ref[...], b_ref[...])


def _mixer_out(o_f, o_r, r, x, norm_w, w_out, g, b, *, tm):
    t, d = x.shape
    dh = o_f.shape[-1]
    dv = norm_w.shape[-1]
    tm = min(tm, t)
    return pl.pallas_call(
        functools.partial(_mixer_out_kernel, dv=dv),
        grid=(t // tm,),
        in_specs=[
            pl.BlockSpec((tm, dh), lambda i: (i, 0)),
            pl.BlockSpec((tm, dh), lambda i: (i, 0)),
            pl.BlockSpec((tm, dh), lambda i: (i, 0)),
            pl.BlockSpec((tm, d), lambda i: (i, 0)),
            _resident(norm_w.shape),
            _resident(w_out.shape),
            _resident(g.shape),
            _resident(b.shape),
        ],
        out_specs=pl.BlockSpec((tm, d), lambda i: (i, 0)),
        out_shape=jax.ShapeDtypeStruct((t, d), F32),
        compiler_params=_cparams("parallel"),
        name="mixer_out_ln",
    )(o_f, o_r, r, x, norm_w, w_out, g, b)


def _gla_proj_kernel(x_ref, win_ref, wgd_ref, wgu_ref, bg_ref,
                     q_ref, k_ref, v_ref, r_ref, g_ref, *, hk, hv):
    xb = x_ref[...].astype(BF16)
    dk = hk // GLA_HEADS
    q_ref[...] = _dot(xb, win_ref[:, 0:hk]) * (dk ** -0.5)
    k_ref[...] = _dot(xb, win_ref[:, hk:2 * hk])
    v_ref[...] = _dot(xb, win_ref[:, 2 * hk:2 * hk + hv])
    r_ref[...] = _dot(xb, win_ref[:, 2 * hk + hv:2 * hk + 2 * hv])
    low = _dot(xb, wgd_ref[...]).astype(BF16)
    logit = _dot(low, wgu_ref[...]) + bg_ref[...]
    logsig = jnp.minimum(logit, 0.0) - jnp.log1p(jnp.exp(-jnp.abs(logit)))
    gate = logsig / GLA_GATE_NORMALIZER
    g_ref[0] = gate[:, 0:hk]
    g_ref[1] = gate[:, hk:2 * hk]


def _gla_proj(x, w_in, w_gd, w_gu, b_g, *, tm):
    t, d = x.shape
    hk = w_gu.shape[1] // 2
    hv = (w_in.shape[1] - 2 * hk) // 2
    tm = min(tm, t)
    row = lambda n: pl.BlockSpec((tm, n), lambda i: (i, 0))
    return pl.pallas_call(
        functools.partial(_gla_proj_kernel, hk=hk, hv=hv),
        grid=(t // tm,),
        in_specs=[row(d), _resident(w_in.shape), _resident(w_gd.shape),
                  _resident(w_gu.shape), _resident(b_g.shape)],
        out_specs=[row(hk), row(hk), row(hv), row(hv),
                   pl.BlockSpec((2, tm, hk), lambda i: (0, i, 0))],
        out_shape=[
            jax.ShapeDtypeStruct((t, hk), F32),
            jax.ShapeDtypeStruct((t, hk), F32),
            jax.ShapeDtypeStruct((t, hv), F32),
            jax.ShapeDtypeStruct((t, hv), F32),
            jax.ShapeDtypeStruct((2, t, hk), F32),
        ],
        compiler_params=_cparams("parallel"),
        name="gla_proj",
    )(x, w_in, w_gd, w_gu, b_g)


def _gla_scan_kernel(q_ref, k_ref, v_ref, g_ref, o_ref, st_ref, *, reverse):
    @pl.when(pl.program_id(2) == 0)
    def _():
        st_ref[...] = jnp.zeros(st_ref.shape, F32)

    cb, dk = q_ref.shape[1], q_ref.shape[2]
    dv = v_ref.shape[2]
    nc = cb // CHUNK
    b = _chunk_cumsum(g_ref[0, 0], 0, reverse).reshape(nc, CHUNK, dk)
    q = q_ref[0].reshape(nc, CHUNK, dk)
    k = k_ref[0].reshape(nc, CHUNK, dk)
    vb = v_ref[0].astype(BF16).reshape(nc, CHUNK, dv)
    mid = CHUNK // 2 if reverse else CHUNK // 2 - 1
    last = 0 if reverse else CHUNK - 1
    b_mid = b[:, mid:mid + 1, :]
    b_last = b[:, last:last + 1, :]

    qe = (q * jnp.exp(b - b_mid)).astype(BF16)
    ke = (k * jnp.exp(b_mid - b)).astype(BF16)
    scores = _bmm_nt(qe, ke)
    row = lax.broadcasted_iota(jnp.int32, (CHUNK, CHUNK), 0)
    col = lax.broadcasted_iota(jnp.int32, (CHUNK, CHUNK), 1)
    visible = (col >= row) if reverse else (col <= row)
    o = _bmm(jnp.where(visible, scores, 0.0).astype(BF16), vb)

    q_start = (q * jnp.exp(b)).astype(BF16)
    k_end = (k * jnp.exp(b_last - b)).astype(BF16)
    decay = jnp.exp(b_last)
    zt = _bmm_tn(vb, k_end)
    st = st_ref[...]
    states = [None] * nc
    for i in (range(nc - 1, -1, -1) if reverse else range(nc)):
        states[i] = st.astype(BF16)
        st = st * decay[i] + zt[i]
    st_ref[...] = st
    o = o + _bmm_nt(q_start, jnp.stack(states))
    o_ref[0] = o.reshape(cb, dv)


def _gla_scan(q, k, v, g, *, direction, cb):
    bsz, s, hk = q.shape
    hv = v.shape[-1]
    dk, dv = hk // GLA_HEADS, hv // GLA_HEADS
    cb = min(cb, s)
    nb = s // cb
    reverse = direction == 1
    blk = (lambda n: nb - 1 - n) if reverse else (lambda n: n)
    return pl.pallas_call(
        functools.partial(_gla_scan_kernel, reverse=reverse),
        grid=(bsz, GLA_HEADS, nb),
        in_specs=[
            pl.BlockSpec((1, cb, dk), lambda bi, h, n: (bi, blk(n), h)),
            pl.BlockSpec((1, cb, dk), lambda bi, h, n: (bi, blk(n), h)),
            pl.BlockSpec((1, cb, dv), lambda bi, h, n: (bi, blk(n), h)),
            pl.BlockSpec((1, 1, cb, dk), lambda bi, h, n: (direction, bi, blk(n), h)),
        ],
        out_specs=pl.BlockSpec((1, cb, dv), lambda bi, h, n: (bi, blk(n), h)),
        out_shape=jax.ShapeDtypeStruct((bsz, s, hv), F32),
        scratch_shapes=[pltpu.VMEM((dv, dk), F32)],
        compiler_params=_cparams("parallel", "parallel", "arbitrary"),
        name="gla_scan_rev" if reverse else "gla_scan_fwd",
    )(q, k, v, g)


def _gdn_proj_kernel(x_ref, xp_ref, xn_ref, win_ref, cw_ref, wab_ref, wabt_ref,
                     alog_ref, dtb_ref, alogt_ref, dtbt_ref,
                     q_ref, k_ref, v_ref, z_ref, gc_ref, beta_ref, gct_ref, p_ref, *, n_qk, n_v):
    i = pl.program_id(1)
    tm = x_ref.shape[1]
    xb = x_ref[0].astype(BF16)
    keep_prev = (i > 0).astype(F32)
    keep_next = (i < pl.num_programs(1) - 1).astype(F32)
    xx = jnp.concatenate([(xp_ref[0] * keep_prev).astype(BF16), xb,
                          (xn_ref[0] * keep_next).astype(BF16)], axis=0)
    n_conv = 2 * n_qk + n_v
    p_ref[...] = _dot(xx, win_ref[:, 0:n_conv])
    z_ref[0] = _dot(xb, win_ref[:, n_conv:n_conv + n_v])

    for j in range(n_conv // GDN_DK):
        cs = slice(j * GDN_DK, (j + 1) * GDN_DK)
        y = jnp.zeros((tm, GDN_DK), F32)
        for tap in range(GDN_CONV):
            y = y + cw_ref[tap:tap + 1, cs] * p_ref[pl.ds(HALO - 2 + tap, tm), cs]
        y = _silu(y)
        if j < 2 * GDN_QK_HEADS:
            y = y * lax.rsqrt(jnp.sum(y * y, axis=-1, keepdims=True) + NORM_EPS)
            if j < GDN_QK_HEADS:
                q_ref[0, :, cs] = y * (GDN_DK ** -0.5)
            else:
                k_ref[0, :, (j - GDN_QK_HEADS) * GDN_DK:(j - GDN_QK_HEADS + 1) * GDN_DK] = y
        else:
            v_ref[0, :, (j - 2 * GDN_QK_HEADS) * GDN_DK:(j - 2 * GDN_QK_HEADS + 1) * GDN_DK] = y

    hv = GDN_V_HEADS
    ab = _dot(xb, wab_ref[...])
    abt = _dot_nt(wabt_ref[...], xb)
    for d in range(2):
        a = ab[:, 2 * d * hv:(2 * d + 1) * hv]
        bt = ab[:, (2 * d + 1) * hv:(2 * d + 2) * hv]
        g = -jnp.exp(alog_ref[d:d + 1, :]) * _softplus(a + dtb_ref[d:d + 1, :])
        gc_ref[d, 0] = _chunk_cumsum(g, 0, d == 1)
        beta_ref[d, 0] = _sigmoid(bt)
        at = abt[2 * d * hv:(2 * d + 1) * hv, :]
        gt = -jnp.exp(alogt_ref[d]) * _softplus(at + dtbt_ref[d])
        gct = _chunk_cumsum(gt, 1, d == 1)
        for c in range(tm // CHUNK):
            gct_ref[d, 0, c] = gct[:, c * CHUNK:(c + 1) * CHUNK]


def _gdn_proj(x, w_in, conv_w, w_ab, w_abt, a_log, dt_bias, a_log_t, dt_bias_t, *, tm):
    bsz, s, d = x.shape
    hv = GDN_V_HEADS
    n_qk = GDN_QK_HEADS * GDN_DK
    n_v = hv * GDN_DK
    tm = min(tm, s)
    nt = s // tm
    hb = tm // HALO
    tok = lambda n: pl.BlockSpec((1, tm, n), lambda bi, i: (bi, i, 0))
    gate = pl.BlockSpec((2, 1, tm, hv), lambda bi, i: (0, bi, i, 0))
    return pl.pallas_call(
        functools.partial(_gdn_proj_kernel, n_qk=n_qk, n_v=n_v),
        grid=(bsz, nt),
        in_specs=[
            tok(d),
            pl.BlockSpec((1, HALO, d), lambda bi, i: (bi, jnp.maximum(i * hb - 1, 0), 0)),
            pl.BlockSpec((1, HALO, d), lambda bi, i: (bi, jnp.minimum((i + 1) * hb, s // HALO - 1), 0)),
            _resident(w_in.shape), _resident(conv_w.shape), _resident(w_ab.shape), _resident(w_abt.shape),
            _resident(a_log.shape), _resident(dt_bias.shape),
            _resident(a_log_t.shape), _resident(dt_bias_t.shape),
        ],
        out_specs=[tok(n_qk), tok(n_qk), tok(n_v), tok(n_v), gate, gate,
                   pl.BlockSpec((2, 1, tm // CHUNK, hv, CHUNK), lambda bi, i: (0, bi, i, 0, 0))],
        out_shape=[
            jax.ShapeDtypeStruct((bsz, s, n_qk), F32),
            jax.ShapeDtypeStruct((bsz, s, n_qk), F32),
            jax.ShapeDtypeStruct((bsz, s, n_v), F32),
            jax.ShapeDtypeStruct((bsz, s, n_v), F32),
            jax.ShapeDtypeStruct((2, bsz, s, hv), F32),
            jax.ShapeDtypeStruct((2, bsz, s, hv), F32),
            jax.ShapeDtypeStruct((2, bsz, s // CHUNK, hv, CHUNK), F32),
        ],
        scratch_shapes=[pltpu.VMEM((tm + 2 * HALO, 2 * n_qk + n_v), F32)],
        compiler_params=_cparams("parallel", "parallel"),
        name="gdn_proj",
    )(x, x, x, w_in, conv_w, w_ab, w_abt, a_log, dt_bias, a_log_t, dt_bias_t)


def _split3(a):
    hi = a.astype(BF16)
    lo = (a - hi.astype(F32)).astype(BF16)
    return hi, lo


def _bmm3(a, b):
    ah, al = _split3(a)
    bh, bl = _split3(b)
    return _bmm(ah, bh) + (_bmm(ah, bl) + _bmm(al, bh))


def _unit_tri_inverse(a):
    row = lax.broadcasted_iota(jnp.int32, (CHUNK, CHUNK), 0)
    col = lax.broadcasted_iota(jnp.int32, (CHUNK, CHUNK), 1)
    same = lambda s: (row // s) == (col // s)
    eye = (row == col).astype(F32)
    ad = jnp.where(same(8), a, 0.0)
    a2 = _bmm3(ad, ad)
    a4 = _bmm3(a2, a2)
    p1 = eye - ad + a2 - _bmm3(ad, a2)
    t = p1 + _bmm3(p1, a4)
    s = 8
    while s < CHUNK:
        off = jnp.logical_and(same(2 * s), jnp.logical_not(same(s)))
        x = _bmm3(jnp.where(off, a, 0.0), t)
        t = t - _bmm3(t, x)
        s *= 2
    return t


def _gdn_scan_kernel(q_ref, k_ref, v_ref, gc_ref, beta_ref, gct_ref, o_ref, st_ref, *, reverse):
    @pl.when(pl.program_id(2) == 0)
    def _():
        st_ref[...] = jnp.zeros(st_ref.shape, F32)

    pair = pl.program_id(1)
    cb, dk = q_ref.shape[1], q_ref.shape[2]
    heads = v_ref.shape[2] // dk
    nc = cb // CHUNK
    q = q_ref[0].reshape(nc, CHUNK, dk)
    k = k_ref[0].reshape(nc, CHUNK, dk)
    qb = q.astype(BF16)
    kb = k.astype(BF16)
    kk = _bmm_nt(kb, kb)
    qk = _bmm_nt(qb, kb)
    row = lax.broadcasted_iota(jnp.int32, (CHUNK, CHUNK), 0)
    col = lax.broadcasted_iota(jnp.int32, (CHUNK, CHUNK), 1)
    incl = (col >= row) if reverse else (col <= row)
    strict = (col > row) if reverse else (col < row)
    last = 0 if reverse else CHUNK - 1
    lane = lax.broadcasted_iota(jnp.int32, gc_ref.shape[2:], 1)
    gc_all = gc_ref[0, 0]
    beta_all = beta_ref[0, 0]

    prep = []
    for hh in range(heads):
        j = pair * heads + hh
        pick = lane == j
        gcol = jnp.sum(jnp.where(pick, gc_all, 0.0), axis=1, keepdims=True).reshape(nc, CHUNK, 1)
        bcol = jnp.sum(jnp.where(pick, beta_all, 0.0), axis=1, keepdims=True).reshape(nc, CHUNK, 1)
        grow = gct_ref[0, 0, :, pl.ds(j, 1), :]
        decay = jnp.where(incl, jnp.exp(jnp.where(incl, gcol - grow, 0.0)), 0.0)
        a = jnp.where(strict, bcol * kk * decay, 0.0)
        attn = (qk * decay).astype(BF16)
        t = _unit_tri_inverse(a).astype(BF16)
        v = v_ref[0, :, hh * dk:(hh + 1) * dk].reshape(nc, CHUNK, dk)
        eg = jnp.exp(gcol)
        rhs = jnp.concatenate([v * bcol, k * (bcol * eg)], axis=-1).astype(BF16)
        uw = _bmm(t, rhs)
        glast = gcol[:, last:last + 1, :]
        q_start = (q * eg).astype(BF16)
        k_end = (k * jnp.exp(glast - gcol)).astype(BF16)
        wq = jnp.concatenate([uw[:, :, dk:].astype(BF16), q_start], axis=1)
        prep.append((uw[:, :, :dk], wq, attn, k_end, jnp.exp(glast)))

    order = range(nc - 1, -1, -1) if reverse else range(nc)
    for hh in range(heads):
        u, wq, attn, k_end, cdec = prep[hh]
        st = st_ref[hh]
        outs = [None] * nc
        for i in order:
            ws = _dot(wq[i], st.astype(BF16))
            v_new = (u[i] - ws[:CHUNK]).astype(BF16)
            outs[i] = ws[CHUNK:] + _dot(attn[i], v_new)
            st = st * cdec[i] + _dot_tn(k_end[i], v_new)
        st_ref[hh] = st
        o_ref[0, :, hh * dk:(hh + 1) * dk] = jnp.concatenate(outs, axis=0)


def _gdn_scan(q, k, v, gc, beta, gct, *, direction, cb, heads):
    bsz, s, n_qk = q.shape
    n_v = v.shape[-1]
    dk = GDN_DK
    hv = n_v // dk
    rep = hv // (n_qk // dk)
    cb = min(cb, s)
    nb = s // cb
    nc = cb // CHUNK
    reverse = direction == 1
    blk = (lambda n: nb - 1 - n) if reverse else (lambda n: n)
    return pl.pallas_call(
        functools.partial(_gdn_scan_kernel, reverse=reverse),
        grid=(bsz, hv // heads, nb),
        in_specs=[
            pl.BlockSpec((1, cb, dk), lambda bi, p, n: (bi, blk(n), p * heads // rep)),
            pl.BlockSpec((1, cb, dk), lambda bi, p, n: (bi, blk(n), p * heads // rep)),
            pl.BlockSpec((1, cb, heads * dk), lambda bi, p, n: (bi, blk(n), p)),
            pl.BlockSpec((1, 1, cb, hv), lambda bi, p, n: (direction, bi, blk(n), 0)),
            pl.BlockSpec((1, 1, cb, hv), lambda bi, p, n: (direction, bi, blk(n), 0)),
            pl.BlockSpec((1, 1, nc, hv, CHUNK), lambda bi, p, n: (direction, bi, blk(n), 0, 0)),
        ],
        out_specs=pl.BlockSpec((1, cb, heads * dk), lambda bi, p, n: (bi, blk(n), p)),
        out_shape=jax.ShapeDtypeStruct((bsz, s, n_v), F32),
        scratch_shapes=[pltpu.VMEM((heads, dk, dk), F32)],
        compiler_params=_cparams("parallel", "parallel", "arbitrary"),
        name="gdn_scan_rev" if reverse else "gdn_scan_fwd",
    )(q, k, v, gc, beta, gct)


def _prepare_weights(ffn_w_in, ffn_w_out, ln_g, ln_b,
                     gla_w_in, gla_w_gate_down, gla_w_gate_up, gla_b_gate, gla_norm_w, gla_w_out,
                     gdn_w_in, gdn_conv_w, gdn_w_ab, gdn_a_log, gdn_dt_bias, gdn_norm_w, gdn_w_out,
                     xa_w_q, xa_w_kv, xa_w_o):
    n_gla, _, d, rank = gla_w_gate_down.shape
    hk = gla_w_gate_up.shape[-1]
    w_gd = jnp.transpose(gla_w_gate_down, (0, 2, 1, 3)).reshape(n_gla, d, 2 * rank)
    zeros = jnp.zeros((n_gla, rank, hk), F32)
    w_gu = jnp.concatenate([
        jnp.concatenate([gla_w_gate_up[:, 0], zeros], axis=-1),
        jnp.concatenate([zeros, gla_w_gate_up[:, 1]], axis=-1)], axis=1)
    n_gdn = gdn_w_ab.shape[0]
    w_ab = jnp.transpose(gdn_w_ab, (0, 2, 1, 3)).reshape(n_gdn, d, -1)
    return dict(
        ffn_w_in=ffn_w_in.astype(BF16), ffn_w_out=ffn_w_out.astype(BF16),
        ln_g=ln_g[:, :, None, :], ln_b=ln_b[:, :, None, :],
        gla_w_in=gla_w_in.astype(BF16), gla_w_gd=w_gd.astype(BF16), gla_w_gu=w_gu.astype(BF16),
        gla_b_g=gla_b_gate.reshape(n_gla, 1, 2 * hk), gla_norm_w=gla_norm_w[:, None, :],
        gla_w_out=gla_w_out.astype(BF16),
        gdn_w_in=gdn_w_in.astype(BF16), gdn_conv_w=gdn_conv_w,
        gdn_w_ab=w_ab.astype(BF16), gdn_w_abt=jnp.transpose(w_ab, (0, 2, 1)).astype(BF16),
        gdn_a_log=gdn_a_log, gdn_dt_bias=gdn_dt_bias,
        gdn_a_log_t=gdn_a_log[..., None], gdn_dt_bias_t=gdn_dt_bias[..., None],
        gdn_norm_w=gdn_norm_w[:, None, :], gdn_w_out=gdn_w_out.astype(BF16),
        xa_w_q=xa_w_q.astype(BF16), xa_w_kv=xa_w_kv.astype(BF16), xa_w_o=xa_w_o.astype(BF16),
    )


def _trunk(x, mem, w):
    bsz, s, d = x.shape
    t = bsz * s
    tm = 512
    cb = 512
    for i in range(DEPTH):
        ln = lambda n: (w['ln_g'][i, n], w['ln_b'][i, n])
        x2 = _ffn_ln(x.reshape(t, d), w['ffn_w_in'][i, 0], w['ffn_w_out'][i, 0], *ln(0), tm=tm)
        j = i // N_MIXERS
        if i % N_MIXERS == 0:
            q, k, v, r, g = _gla_proj(x2, w['gla_w_in'][j], w['gla_w_gd'][j], w['gla_w_gu'][j],
                                      w['gla_b_g'][j], tm=tm)
            q, k, v = (a.reshape(bsz, s, -1) for a in (q, k, v))
            g = g.reshape(2, bsz, s, -1)
            o_f = _gla_scan(q, k, v, g, direction=0, cb=cb)
            o_r = _gla_scan(q, k, v, g, direction=1, cb=cb)
            norm_w, w_out = w['gla_norm_w'][j], w['gla_w_out'][j]
        else:
            q, k, v, r, gc, beta, gct = _gdn_proj(
                x2.reshape(bsz, s, d), w['gdn_w_in'][j], w['gdn_conv_w'][j], w['gdn_w_ab'][j],
                w['gdn_w_abt'][j], w['gdn_a_log'][j], w['gdn_dt_bias'][j],
                w['gdn_a_log_t'][j], w['gdn_dt_bias_t'][j], tm=tm)
            o_f = _gdn_scan(q, k, v, gc, beta, gct, direction=0, cb=cb, heads=2)
            o_r = _gdn_scan(q, k, v, gc, beta, gct, direction=1, cb=cb, heads=2)
            norm_w, w_out = w['gdn_norm_w'][j], w['gdn_w_out'][j]
        dh = o_f.shape[-1]
        x2 = _mixer_out(o_f.reshape(t, dh), o_r.reshape(t, dh), r.reshape(t, dh), x2,
                        norm_w, w_out, *ln(1), tm=tm)
        kv = _proj(mem.reshape(-1, d), w['xa_w_kv'][i], tm=256, out_dtype=BF16)
        x3 = _xattn_ln(x2.reshape(bsz, s, d), kv.reshape(bsz, -1, 2 * d),
                       w['xa_w_q'][i], w['xa_w_o'][i], *ln(2), tm=tm)
        x = _ffn_ln(x3.reshape(t, d), w['ffn_w_in'][i, 1], w['ffn_w_out'][i, 1], *ln(3), tm=tm)
        x = x.reshape(bsz, s, d)
    return x


def kernel(x_prompt, x_sample, mem_prompt, mem_sample, ffn_w_in, ffn_w_out, ln_g, ln_b, gla_w_in, gla_w_gate_down, gla_w_gate_up, gla_b_gate, gla_norm_w, gla_w_out, gdn_w_in, gdn_conv_w, gdn_w_ab, gdn_a_log, gdn_dt_bias, gdn_norm_w, gdn_w_out, xa_w_q, xa_w_kv, xa_w_o):
    w = _prepare_weights(ffn_w_in, ffn_w_out, ln_g, ln_b,
                         gla_w_in, gla_w_gate_down, gla_w_gate_up, gla_b_gate, gla_norm_w, gla_w_out,
                         gdn_w_in, gdn_conv_w, gdn_w_ab, gdn_a_log, gdn_dt_bias, gdn_norm_w, gdn_w_out,
                         xa_w_q, xa_w_kv, xa_w_o)
    return (_trunk(x_prompt, mem_prompt, w), _trunk(x_sample, mem_sample, w))
```

```python
import functools

import jax
import jax.numpy as jnp
from jax import lax
from jax.experimental import pallas as pl
from jax.experimental.pallas import tpu as pltpu

F32 = jnp.float32
BF16 = jnp.bfloat16

DEPTH = 2
N_MIXERS = 2
CHUNK = 64
GLA_HEADS = 4
GLA_GATE_NORMALIZER = 16.0
GDN_QK_HEADS = 8
GDN_V_HEADS = 16
GDN_DK = 128
GDN_CONV = 4
XA_HEADS = 4
ALPHA = (2.0 * DEPTH) ** 0.25
LN_EPS = 1e-5
NORM_EPS = 1e-6

VMEM_LIMIT_BYTES = 56 * 1024 * 1024
HALO = 8
PREP_QK_HEADS = 2


def _cparams(*sem):
    return pltpu.CompilerParams(dimension_semantics=sem, vmem_limit_bytes=VMEM_LIMIT_BYTES)


def _resident(shape):
    nd = len(shape)
    return pl.BlockSpec(shape, lambda *_: (0,) * nd, pipeline_mode=pl.Buffered(1))


def _dot(a, b):
    return jnp.dot(a, b, preferred_element_type=F32)


def _dot_nt(a, b):
    return lax.dot_general(a, b, (((1,), (1,)), ((), ())), preferred_element_type=F32)


def _dot_tn(a, b):
    return lax.dot_general(a, b, (((0,), (0,)), ((), ())), preferred_element_type=F32)


def _bmm(a, b):
    return jnp.einsum('nmk,nkp->nmp', a, b, preferred_element_type=F32)


def _bmm_nt(a, b):
    return jnp.einsum('nmk,npk->nmp', a, b, preferred_element_type=F32)


def _bmm_tn(a, b):
    return jnp.einsum('nkm,nkp->nmp', a, b, preferred_element_type=F32)


def _sigmoid(x):
    return jax.nn.sigmoid(x)


def _silu(x):
    return x * _sigmoid(x)


def _softplus(x):
    return jnp.maximum(x, 0.0) + jnp.log1p(jnp.exp(-jnp.abs(x)))


def _layer_norm(y, g, b):
    mu = jnp.mean(y, axis=-1, keepdims=True)
    yc = y - mu
    var = jnp.mean(yc * yc, axis=-1, keepdims=True)
    return yc * lax.rsqrt(var + LN_EPS) * g + b


def _chunk_cumsum(x, axis, reverse):
    n = x.shape[axis]
    pos = lax.broadcasted_iota(jnp.int32, x.shape, axis) % CHUNK
    s = 1
    while s < CHUNK:
        if reverse:
            shifted = pltpu.roll(x, n - s, axis)
            keep = pos < CHUNK - s
        else:
            shifted = pltpu.roll(x, s, axis)
            keep = pos >= s
        x = x + jnp.where(keep, shifted, 0.0)
        s *= 2
    return x


def _ffn_ln_kernel(x_ref, win_ref, wout_ref, g_ref, b_ref, o_ref, *, d_ff, n_split):
    x = x_ref[...]
    xb = x.astype(BF16)
    fc = d_ff // n_split
    acc = jnp.zeros(x.shape, F32)
    for c in range(n_split):
        lo = c * fc
        gate = _dot(xb, win_ref[:, lo:lo + fc])
        up = _dot(xb, win_ref[:, d_ff + lo:d_ff + lo + fc])
        h = (_silu(gate) * up).astype(BF16)
        acc = acc + _dot(h, wout_ref[lo:lo + fc, :])
    y = ALPHA * x + 0.5 * acc
    o_ref[...] = _layer_norm(y, g_ref[...], b_ref[...])


def _ffn_ln(x, w_in, w_out, g, b, *, tm):
    t, d = x.shape
    d_ff = w_out.shape[0]
    tm = min(tm, t)
    return pl.pallas_call(
        functools.partial(_ffn_ln_kernel, d_ff=d_ff, n_split=2),
        grid=(t // tm,),
        in_specs=[
            pl.BlockSpec((tm, d), lambda i: (i, 0)),
            _resident(w_in.shape),
            _resident(w_out.shape),
            _resident(g.shape),
            _resident(b.shape),
        ],
        out_specs=pl.BlockSpec((tm, d), lambda i: (i, 0)),
        out_shape=jax.ShapeDtypeStruct((t, d), F32),
        compiler_params=_cparams("parallel"),
        name="ffn_ln",
    )(x, w_in, w_out, g, b)


def _proj_kernel(x_ref, w_ref, o_ref):
    o_ref[...] = _dot(x_ref[...].astype(BF16), w_ref[...]).astype(o_ref.dtype)


def _proj(x, w, *, tm, out_dtype):
    t, d = x.shape
    n = w.shape[1]
    tm = min(tm, t)
    return pl.pallas_call(
        _proj_kernel,
        grid=(t // tm,),
        in_specs=[pl.BlockSpec((tm, d), lambda i: (i, 0)), _resident(w.shape)],
        out_specs=pl.BlockSpec((tm, n), lambda i: (i, 0)),
        out_shape=jax.ShapeDtypeStruct((t, n), out_dtype),
        compiler_params=_cparams("parallel"),
        name="mem_kv_proj",
    )(x, w)


def _xattn_ln_kernel(x_ref, kv_ref, wq_ref, wo_ref, g_ref, b_ref, o_ref, *, heads):
    x = x_ref[0]
    d = x.shape[-1]
    dh = d // heads
    q = _dot(x.astype(BF16), wq_ref[...]).astype(BF16)
    outs = []
    for h in range(heads):
        qh = q[:, h * dh:(h + 1) * dh]
        kh = kv_ref[0, :, h * dh:(h + 1) * dh]
        vh = kv_ref[0, :, d + h * dh:d + (h + 1) * dh]
        s = _dot_nt(qh, kh) * (dh ** -0.5)
        e = jnp.exp(s - jnp.max(s, axis=-1, keepdims=True))
        den = jnp.sum(e, axis=-1, keepdims=True)
        outs.append(_dot(e.astype(BF16), vh) / den)
    o = jnp.concatenate(outs, axis=-1).astype(BF16)
    y = ALPHA * x + _dot(o, wo_ref[...])
    o_ref[0] = _layer_norm(y, g_ref[...], b_ref[...])


def _xattn_ln(x, kv, w_q, w_o, g, b, *, tm):
    bsz, s, d = x.shape
    n_mem = kv.shape[1]
    tm = min(tm, s)
    return pl.pallas_call(
        functools.partial(_xattn_ln_kernel, heads=XA_HEADS),
        grid=(bsz, s // tm),
        in_specs=[
            pl.BlockSpec((1, tm, d), lambda bi, i: (bi, i, 0)),
            pl.BlockSpec((1, n_mem, 2 * d), lambda bi, i: (bi, 0, 0)),
            _resident(w_q.shape),
            _resident(w_o.shape),
            _resident(g.shape),
            _resident(b.shape),
        ],
        out_specs=pl.BlockSpec((1, tm, d), lambda bi, i: (bi, i, 0)),
        out_shape=jax.ShapeDtypeStruct((bsz, s, d), F32),
        compiler_params=_cparams("parallel", "parallel"),
        name="xattn_ln",
    )(x, kv, w_q, w_o, g, b)


def _mixer_out_kernel(of_ref, or_ref, r_ref, x_ref, nw_ref, wo_ref, g_ref, b_ref, o_ref, *, dv, head_major):
    nw = nw_ref[...]
    parts = []
    for h in range(r_ref.shape[-1] // dv):
        if head_major:
            oh = of_ref[0, h] + or_ref[0, h]
        else:
            oh = of_ref[0, :, h * dv:(h + 1) * dv] + or_ref[0, :, h * dv:(h + 1) * dv]
        ms = jnp.mean(oh * oh, axis=-1, keepdims=True)
        parts.append(oh * lax.rsqrt(ms + NORM_EPS) * nw)
    gated = (jnp.concatenate(parts, axis=-1) * _silu(r_ref[0])).astype(BF16)
    y = ALPHA * x_ref[0] + _dot(gated, wo_ref[...])
    o_ref[0] = _layer_norm(y, g_ref[...], b_ref[...])


def _mixer_out(o_f, o_r, r, x, norm_w, w_out, g, b, *, tm):
    bsz, s, d = x.shape
    dh = r.shape[-1]
    dv = norm_w.shape[-1]
    tm = min(tm, s)
    head_major = o_f.ndim == 4
    tok = lambda n: pl.BlockSpec((1, tm, n), lambda bi, i: (bi, i, 0))
    o_spec = pl.BlockSpec((1, dh // dv, tm, dv), lambda bi, i: (bi, 0, i, 0)) if head_major else tok(dh)
    return pl.pallas_call(
        functools.partial(_mixer_out_kernel, dv=dv, head_major=head_major),
        grid=(bsz, s // tm),
        in_specs=[
            o_spec, o_spec, tok(dh), tok(d),
            _resident(norm_w.shape),
            _resident(w_out.shape),
            _resident(g.shape),
            _resident(b.shape),
        ],
        out_specs=tok(d),
        out_shape=jax.ShapeDtypeStruct((bsz, s, d), F32),
        compiler_params=_cparams("parallel", "parallel"),
        name="mixer_out_ln",
    )(o_f, o_r, r, x, norm_w, w_out, g, b)


def _gla_proj_kernel(x_ref, win_ref, wgd_ref, wgu_ref, bg_ref,
                     q_ref, k_ref, v_ref, r_ref, g_ref, *, hk, hv):
    xb = x_ref[...].astype(BF16)
    dk = hk // GLA_HEADS
    q_ref[...] = _dot(xb, win_ref[:, 0:hk]) * (dk ** -0.5)
    k_ref[...] = _dot(xb, win_ref[:, hk:2 * hk])
    v_ref[...] = _dot(xb, win_ref[:, 2 * hk:2 * hk + hv])
    r_ref[...] = _dot(xb, win_ref[:, 2 * hk + hv:2 * hk + 2 * hv])
    low = _dot(xb, wgd_ref[...]).astype(BF16)
    logit = _dot(low, wgu_ref[...]) + bg_ref[...]
    logsig = jnp.minimum(logit, 0.0) - jnp.log1p(jnp.exp(-jnp.abs(logit)))
    gate = logsig / GLA_GATE_NORMALIZER
    g_ref[0] = gate[:, 0:hk]
    g_ref[1] = gate[:, hk:2 * hk]


def _gla_proj(x, w_in, w_gd, w_gu, b_g, *, tm):
    t, d = x.shape
    hk = w_gu.shape[1] // 2
    hv = (w_in.shape[1] - 2 * hk) // 2
    tm = min(tm, t)
    row = lambda n: pl.BlockSpec((tm, n), lambda i: (i, 0))
    return pl.pallas_call(
        functools.partial(_gla_proj_kernel, hk=hk, hv=hv),
        grid=(t // tm,),
        in_specs=[row(d), _resident(w_in.shape), _resident(w_gd.shape),
                  _resident(w_gu.shape), _resident(b_g.shape)],
        out_specs=[row(hk), row(hk), row(hv), row(hv),
                   pl.BlockSpec((2, tm, hk), lambda i: (0, i, 0))],
        out_shape=[
            jax.ShapeDtypeStruct((t, hk), F32),
            jax.ShapeDtypeStruct((t, hk), F32),
            jax.ShapeDtypeStruct((t, hv), F32),
            jax.ShapeDtypeStruct((t, hv), F32),
            jax.ShapeDtypeStruct((2, t, hk), F32),
        ],
        compiler_params=_cparams("parallel"),
        name="gla_proj",
    )(x, w_in, w_gd, w_gu, b_g)


def _gla_scan_kernel(q_ref, k_ref, v_ref, g_ref, o_ref, st_ref, *, reverse):
    @pl.when(pl.program_id(2) == 0)
    def _():
        st_ref[...] = jnp.zeros(st_ref.shape, F32)

    cb, dk = q_ref.shape[1], q_ref.shape[2]
    dv = v_ref.shape[2]
    nc = cb // CHUNK
    b = _chunk_cumsum(g_ref[0, 0], 0, reverse).reshape(nc, CHUNK, dk)
    q = q_ref[0].reshape(nc, CHUNK, dk)
    k = k_ref[0].reshape(nc, CHUNK, dk)
    vb = v_ref[0].astype(BF16).reshape(nc, CHUNK, dv)
    mid = CHUNK // 2 if reverse else CHUNK // 2 - 1
    last = 0 if reverse else CHUNK - 1
    b_mid = b[:, mid:mid + 1, :]
    b_last = b[:, last:last + 1, :]

    qe = (q * jnp.exp(b - b_mid)).astype(BF16)
    ke = (k * jnp.exp(b_mid - b)).astype(BF16)
    scores = _bmm_nt(qe, ke)
    row = lax.broadcasted_iota(jnp.int32, (CHUNK, CHUNK), 0)
    col = lax.broadcasted_iota(jnp.int32, (CHUNK, CHUNK), 1)
    visible = (col >= row) if reverse else (col <= row)
    o = _bmm(jnp.where(visible, scores, 0.0).astype(BF16), vb)

    q_start = (q * jnp.exp(b)).astype(BF16)
    k_end = (k * jnp.exp(b_last - b)).astype(BF16)
    decay = jnp.exp(b_last)
    zt = _bmm_tn(vb, k_end)
    st = st_ref[...]
    states = [None] * nc
    for i in (range(nc - 1, -1, -1) if reverse else range(nc)):
        states[i] = st.astype(BF16)
        st = st * decay[i] + zt[i]
    st_ref[...] = st
    o = o + _bmm_nt(q_start, jnp.stack(states))
    o_ref[0] = o.reshape(cb, dv)


def _gla_scan(q, k, v, g, *, direction, cb):
    bsz, s, hk = q.shape
    hv = v.shape[-1]
    dk, dv = hk // GLA_HEADS, hv // GLA_HEADS
    cb = min(cb, s)
    nb = s // cb
    reverse = direction == 1
    blk = (lambda n: nb - 1 - n) if reverse else (lambda n: n)
    return pl.pallas_call(
        functools.partial(_gla_scan_kernel, reverse=reverse),
        grid=(bsz, GLA_HEADS, nb),
        in_specs=[
            pl.BlockSpec((1, cb, dk), lambda bi, h, n: (bi, blk(n), h)),
            pl.BlockSpec((1, cb, dk), lambda bi, h, n: (bi, blk(n), h)),
            pl.BlockSpec((1, cb, dv), lambda bi, h, n: (bi, blk(n), h)),
            pl.BlockSpec((1, 1, cb, dk), lambda bi, h, n: (direction, bi, blk(n), h)),
        ],
        out_specs=pl.BlockSpec((1, cb, dv), lambda bi, h, n: (bi, blk(n), h)),
        out_shape=jax.ShapeDtypeStruct((bsz, s, hv), F32),
        scratch_shapes=[pltpu.VMEM((dv, dk), F32)],
        compiler_params=_cparams("parallel", "parallel", "arbitrary"),
        name="gla_scan_rev" if reverse else "gla_scan_fwd",
    )(q, k, v, g)


def _gdn_proj_kernel(x_ref, xp_ref, xn_ref, win_ref, cw_ref, wab_ref, wabt_ref,
                     alog_ref, dtb_ref, alogt_ref, dtbt_ref,
                     q_ref, k_ref, v_ref, z_ref, gc_ref, beta_ref, gct_ref, p_ref, *, n_qk, n_v):
    i = pl.program_id(1)
    tm = x_ref.shape[1]
    xb = x_ref[0].astype(BF16)
    keep_prev = (i > 0).astype(F32)
    keep_next = (i < pl.num_programs(1) - 1).astype(F32)
    xx = jnp.concatenate([(xp_ref[0] * keep_prev).astype(BF16), xb,
                          (xn_ref[0] * keep_next).astype(BF16)], axis=0)
    n_conv = 2 * n_qk + n_v
    p_ref[...] = _dot(xx, win_ref[:, 0:n_conv])
    z_ref[0] = _dot(xb, win_ref[:, n_conv:n_conv + n_v])

    for j in range(n_conv // GDN_DK):
        cs = slice(j * GDN_DK, (j + 1) * GDN_DK)
        y = jnp.zeros((tm, GDN_DK), F32)
        for tap in range(GDN_CONV):
            y = y + cw_ref[tap:tap + 1, cs] * p_ref[pl.ds(HALO - 2 + tap, tm), cs]
        y = _silu(y)
        if j < 2 * GDN_QK_HEADS:
            y = y * lax.rsqrt(jnp.sum(y * y, axis=-1, keepdims=True) + NORM_EPS)
            if j < GDN_QK_HEADS:
                q_ref[0, j] = y * (GDN_DK ** -0.5)
            else:
                k_ref[0, j - GDN_QK_HEADS] = y
        else:
            v_ref[0, j - 2 * GDN_QK_HEADS] = y

    hv = GDN_V_HEADS
    ab = _dot(xb, wab_ref[...])
    abt = _dot_nt(wabt_ref[...], xb)
    for d in range(2):
        a = ab[:, 2 * d * hv:(2 * d + 1) * hv]
        bt = ab[:, (2 * d + 1) * hv:(2 * d + 2) * hv]
        g = -jnp.exp(alog_ref[d:d + 1, :]) * _softplus(a + dtb_ref[d:d + 1, :])
        gc_ref[d, 0] = _chunk_cumsum(g, 0, d == 1)
        beta_ref[d, 0] = _sigmoid(bt)
        at = abt[2 * d * hv:(2 * d + 1) * hv, :]
        gt = -jnp.exp(alogt_ref[d]) * _softplus(at + dtbt_ref[d])
        gct = _chunk_cumsum(gt, 1, d == 1)
        for c in range(tm // CHUNK):
            gct_ref[d, 0, c] = gct[:, c * CHUNK:(c + 1) * CHUNK]


def _gdn_proj(x, w_in, conv_w, w_ab, w_abt, a_log, dt_bias, a_log_t, dt_bias_t, *, tm):
    bsz, s, d = x.shape
    hv = GDN_V_HEADS
    n_qk = GDN_QK_HEADS * GDN_DK
    n_v = hv * GDN_DK
    tm = min(tm, s)
    nt = s // tm
    hb = tm // HALO
    tok = lambda n: pl.BlockSpec((1, tm, n), lambda bi, i: (bi, i, 0))
    head = lambda n: pl.BlockSpec((1, n, tm, GDN_DK), lambda bi, i: (bi, 0, i, 0))
    gate = pl.BlockSpec((2, 1, tm, hv), lambda bi, i: (0, bi, i, 0))
    return pl.pallas_call(
        functools.partial(_gdn_proj_kernel, n_qk=n_qk, n_v=n_v),
        grid=(bsz, nt),
        in_specs=[
            tok(d),
            pl.BlockSpec((1, HALO, d), lambda bi, i: (bi, jnp.maximum(i * hb - 1, 0), 0)),
            pl.BlockSpec((1, HALO, d), lambda bi, i: (bi, jnp.minimum((i + 1) * hb, s // HALO - 1), 0)),
            _resident(w_in.shape), _resident(conv_w.shape), _resident(w_ab.shape), _resident(w_abt.shape),
            _resident(a_log.shape), _resident(dt_bias.shape),
            _resident(a_log_t.shape), _resident(dt_bias_t.shape),
        ],
        out_specs=[head(GDN_QK_HEADS), head(GDN_QK_HEADS), head(hv), tok(n_v), gate, gate,
                   pl.BlockSpec((2, 1, tm // CHUNK, hv, CHUNK), lambda bi, i: (0, bi, i, 0, 0))],
        out_shape=[
            jax.ShapeDtypeStruct((bsz, GDN_QK_HEADS, s, GDN_DK), F32),
            jax.ShapeDtypeStruct((bsz, GDN_QK_HEADS, s, GDN_DK), F32),
            jax.ShapeDtypeStruct((bsz, hv, s, GDN_DK), F32),
            jax.ShapeDtypeStruct((bsz, s, n_v), F32),
            jax.ShapeDtypeStruct((2, bsz, s, hv), F32),
            jax.ShapeDtypeStruct((2, bsz, s, hv), F32),
            jax.ShapeDtypeStruct((2, bsz, s // CHUNK, hv, CHUNK), F32),
        ],
        scratch_shapes=[pltpu.VMEM((tm + 2 * HALO, 2 * n_qk + n_v), F32)],
        compiler_params=_cparams("parallel", "parallel"),
        name="gdn_proj",
    )(x, x, x, w_in, conv_w, w_ab, w_abt, a_log, dt_bias, a_log_t, dt_bias_t)


def _bmm3(a, b):
    return _bmm(a.astype(BF16), b.astype(BF16))


def _unit_tri_inverse(a):
    row = lax.broadcasted_iota(jnp.int32, (CHUNK, CHUNK), 0)
    col = lax.broadcasted_iota(jnp.int32, (CHUNK, CHUNK), 1)
    same = lambda s: (row // s) == (col // s)
    eye = (row == col).astype(F32)
    ad = jnp.where(same(8), a, 0.0)
    a2 = _bmm3(ad, ad)
    a4 = _bmm3(a2, a2)
    p1 = eye - ad + a2 - _bmm3(ad, a2)
    t = p1 + _bmm3(p1, a4)
    s = 8
    while s < CHUNK:
        off = jnp.logical_and(same(2 * s), jnp.logical_not(same(s)))
        x = _bmm3(jnp.where(off, a, 0.0), t)
        t = t - _bmm3(t, x)
        s *= 2
    return t


def _gdn_scan_kernel(q_ref, k_ref, v_ref, gc_ref, beta_ref, gct_ref, o_ref,
                     st_ref, u_ref, wq_ref, attn_ref, ke_ref, cdec_ref, *, reverse):
    @pl.when(pl.program_id(1) == 0)
    def _():
        st_ref[...] = jnp.zeros(st_ref.shape, F32)

    qk_heads, cb, dk = q_ref.shape[1:]
    hv = v_ref.shape[1]
    rep = hv // qk_heads
    nc = cb // CHUNK
    row = lax.broadcasted_iota(jnp.int32, (CHUNK, CHUNK), 0)
    col = lax.broadcasted_iota(jnp.int32, (CHUNK, CHUNK), 1)
    incl = (col >= row) if reverse else (col <= row)
    strict = (col > row) if reverse else (col < row)
    last = 0 if reverse else CHUNK - 1
    lane = lax.broadcasted_iota(jnp.int32, gc_ref.shape[2:], 1)

    def prep(it, carry):
        gc_all = gc_ref[0, 0]
        beta_all = beta_ref[0, 0]
        heads, a_list, rhs_list, late = [], [], [], []
        for pp in range(PREP_QK_HEADS):
            p = it * PREP_QK_HEADS + pp
            q = q_ref[0, p].reshape(nc, CHUNK, dk)
            k = k_ref[0, p].reshape(nc, CHUNK, dk)
            kb = k.astype(BF16)
            kk = _bmm_nt(kb, kb)
            qk = _bmm_nt(q.astype(BF16), kb)
            for hh in range(rep):
                j = p * rep + hh
                pick = lane == j
                gcol = jnp.sum(jnp.where(pick, gc_all, 0.0), axis=1, keepdims=True).reshape(nc, CHUNK, 1)
                bcol = jnp.sum(jnp.where(pick, beta_all, 0.0), axis=1, keepdims=True).reshape(nc, CHUNK, 1)
                grow = gct_ref[0, 0, :, pl.ds(j, 1), :]
                decay = jnp.where(incl, jnp.exp(jnp.where(incl, gcol - grow, 0.0)), 0.0)
                eg = jnp.exp(gcol)
                glast = gcol[:, last:last + 1, :]
                v = v_ref[0, j].reshape(nc, CHUNK, dk)
                heads.append(j)
                a_list.append(jnp.where(strict, bcol * kk * decay, 0.0))
                rhs_list.append(jnp.concatenate([v * bcol, k * (bcol * eg)], axis=-1).astype(BF16))
                late.append(((q * eg).astype(BF16), (qk * decay).astype(BF16),
                             (k * jnp.exp(glast - gcol)).astype(BF16),
                             jnp.broadcast_to(jnp.exp(glast), (nc, 1, dk))))
        t = _unit_tri_inverse(jnp.concatenate(a_list, axis=0)).astype(BF16)
        uw = _bmm(t, jnp.concatenate(rhs_list, axis=0))
        for n, (j, (q_start, attn, k_end, cdec)) in enumerate(zip(heads, late)):
            uw_j = uw[n * nc:(n + 1) * nc]
            u_ref[:, j] = uw_j[:, :, :dk]
            wq_ref[:, j] = jnp.concatenate([uw_j[:, :, dk:].astype(BF16), q_start], axis=1)
            attn_ref[:, j] = attn
            ke_ref[:, j] = k_end
            cdec_ref[:, j] = cdec
        return carry

    lax.fori_loop(0, qk_heads // PREP_QK_HEADS, prep, 0)

    def step(ii, carry):
        i = nc - 1 - ii if reverse else ii
        r0 = pl.multiple_of(i * CHUNK, CHUNK)
        st = st_ref[...]
        ws = _bmm(wq_ref[i], st.astype(BF16))
        v_new = (u_ref[i] - ws[:, :CHUNK]).astype(BF16)
        o_ref[0, :, pl.ds(r0, CHUNK), :] = ws[:, CHUNK:] + _bmm(attn_ref[i], v_new)
        st_ref[...] = st * cdec_ref[i] + _bmm_tn(ke_ref[i], v_new)
        return carry

    lax.fori_loop(0, nc, step, 0)


def _gdn_scan(q, k, v, gc, beta, gct, *, direction, cb):
    bsz, qk_heads, s, dk = q.shape
    hv = v.shape[1]
    cb = min(cb, s)
    nb = s // cb
    nc = cb // CHUNK
    reverse = direction == 1
    blk = (lambda n: nb - 1 - n) if reverse else (lambda n: n)
    head_major = lambda h: pl.BlockSpec((1, h, cb, dk), lambda bi, n: (bi, 0, blk(n), 0))
    gate = pl.BlockSpec((1, 1, cb, hv), lambda bi, n: (direction, bi, blk(n), 0))
    return pl.pallas_call(
        functools.partial(_gdn_scan_kernel, reverse=reverse),
        grid=(bsz, nb),
        in_specs=[
            head_major(qk_heads), head_major(qk_heads), head_major(hv), gate, gate,
            pl.BlockSpec((1, 1, nc, hv, CHUNK), lambda bi, n: (direction, bi, blk(n), 0, 0)),
        ],
        out_specs=head_major(hv),
        out_shape=jax.ShapeDtypeStruct((bsz, hv, s, dk), F32),
        scratch_shapes=[
            pltpu.VMEM((hv, dk, dk), F32),
            pltpu.VMEM((nc, hv, CHUNK, dk), F32),
            pltpu.VMEM((nc, hv, 2 * CHUNK, dk), BF16),
            pltpu.VMEM((nc, hv, CHUNK, CHUNK), BF16),
            pltpu.VMEM((nc, hv, CHUNK, dk), BF16),
            pltpu.VMEM((nc, hv, 1, dk), F32),
        ],
        compiler_params=_cparams("parallel", "arbitrary"),
        name="gdn_scan_rev" if reverse else "gdn_scan_fwd",
    )(q, k, v, gc, beta, gct)


def _prepare_weights(ffn_w_in, ffn_w_out, ln_g, ln_b,
                     gla_w_in, gla_w_gate_down, gla_w_gate_up, gla_b_gate, gla_norm_w, gla_w_out,
                     gdn_w_in, gdn_conv_w, gdn_w_ab, gdn_a_log, gdn_dt_bias, gdn_norm_w, gdn_w_out,
                     xa_w_q, xa_w_kv, xa_w_o):
    n_gla, _, d, rank = gla_w_gate_down.shape
    hk = gla_w_gate_up.shape[-1]
    w_gd = jnp.transpose(gla_w_gate_down, (0, 2, 1, 3)).reshape(n_gla, d, 2 * rank)
    zeros = jnp.zeros((n_gla, rank, hk), F32)
    w_gu = jnp.concatenate([
        jnp.concatenate([gla_w_gate_up[:, 0], zeros], axis=-1),
        jnp.concatenate([zeros, gla_w_gate_up[:, 1]], axis=-1)], axis=1)
    n_gdn = gdn_w_ab.shape[0]
    w_ab = jnp.transpose(gdn_w_ab, (0, 2, 1, 3)).reshape(n_gdn, d, -1)
    return dict(
        ffn_w_in=ffn_w_in.astype(BF16), ffn_w_out=ffn_w_out.astype(BF16),
        ln_g=ln_g[:, :, None, :], ln_b=ln_b[:, :, None, :],
        gla_w_in=gla_w_in.astype(BF16), gla_w_gd=w_gd.astype(BF16), gla_w_gu=w_gu.astype(BF16),
        gla_b_g=gla_b_gate.reshape(n_gla, 1, 2 * hk), gla_norm_w=gla_norm_w[:, None, :],
        gla_w_out=gla_w_out.astype(BF16),
        gdn_w_in=gdn_w_in.astype(BF16), gdn_conv_w=gdn_conv_w,
        gdn_w_ab=w_ab.astype(BF16), gdn_w_abt=jnp.transpose(w_ab, (0, 2, 1)).astype(BF16),
        gdn_a_log=gdn_a_log, gdn_dt_bias=gdn_dt_bias,
        gdn_a_log_t=gdn_a_log[..., None], gdn_dt_bias_t=gdn_dt_bias[..., None],
        gdn_norm_w=gdn_norm_w[:, None, :], gdn_w_out=gdn_w_out.astype(BF16),
        xa_w_q=xa_w_q.astype(BF16), xa_w_kv=xa_w_kv.astype(BF16), xa_w_o=xa_w_o.astype(BF16),
    )


def _trunk(x, mem, w):
    bsz, s, d = x.shape
    t = bsz * s
    tm = 512
    cb = 512
    for i in range(DEPTH):
        ln = lambda n: (w['ln_g'][i, n], w['ln_b'][i, n])
        x2 = _ffn_ln(x.reshape(t, d), w['ffn_w_in'][i, 0], w['ffn_w_out'][i, 0], *ln(0), tm=tm)
        j = i // N_MIXERS
        if i % N_MIXERS == 0:
            q, k, v, r, g = _gla_proj(x2, w['gla_w_in'][j], w['gla_w_gd'][j], w['gla_w_gu'][j],
                                      w['gla_b_g'][j], tm=tm)
            q, k, v, r = (a.reshape(bsz, s, -1) for a in (q, k, v, r))
            g = g.reshape(2, bsz, s, -1)
            o_f = _gla_scan(q, k, v, g, direction=0, cb=cb)
            o_r = _gla_scan(q, k, v, g, direction=1, cb=cb)
            norm_w, w_out = w['gla_norm_w'][j], w['gla_w_out'][j]
        else:
            q, k, v, r, gc, beta, gct = _gdn_proj(
                x2.reshape(bsz, s, d), w['gdn_w_in'][j], w['gdn_conv_w'][j], w['gdn_w_ab'][j],
                w['gdn_w_abt'][j], w['gdn_a_log'][j], w['gdn_dt_bias'][j],
                w['gdn_a_log_t'][j], w['gdn_dt_bias_t'][j], tm=tm)
            o_f = _gdn_scan(q, k, v, gc, beta, gct, direction=0, cb=cb)
            o_r = _gdn_scan(q, k, v, gc, beta, gct, direction=1, cb=cb)
            norm_w, w_out = w['gdn_norm_w'][j], w['gdn_w_out'][j]
        x2 = _mixer_out(o_f, o_r, r, x2.reshape(bsz, s, d), norm_w, w_out, *ln(1), tm=tm)
        kv = _proj(mem.reshape(-1, d), w['xa_w_kv'][i], tm=256, out_dtype=BF16)
        x3 = _xattn_ln(x2, kv.reshape(bsz, -1, 2 * d),
                       w['xa_w_q'][i], w['xa_w_o'][i], *ln(2), tm=tm)
        x = _ffn_ln(x3.reshape(t, d), w['ffn_w_in'][i, 1], w['ffn_w_out'][i, 1], *ln(3), tm=tm)
        x = x.reshape(bsz, s, d)
    return x


def kernel(x_prompt, x_sample, mem_prompt, mem_sample, ffn_w_in, ffn_w_out, ln_g, ln_b, gla_w_in, gla_w_gate_down, gla_w_gate_up, gla_b_gate, gla_norm_w, gla_w_out, gdn_w_in, gdn_conv_w, gdn_w_ab, gdn_a_log, gdn_dt_bias, gdn_norm_w, gdn_w_out, xa_w_q, xa_w_kv, xa_w_o):
    w = _prepare_weights(ffn_w_in, ffn_w_out, ln_g, ln_b,
                         gla_w_in, gla_w_gate_down, gla_w_gate_up, gla_b_gate, gla_norm_w, gla_w_out,
                         gdn_w_in, gdn_conv_w, gdn_w_ab, gdn_a_log, gdn_dt_bias, gdn_norm_w, gdn_w_out,
                         xa_w_q, xa_w_kv, xa_w_o)
    return (_trunk(x_prompt, mem_prompt, w), _trunk(x_sample, mem_sample, w))
```

```python
import functools

import jax
import jax.numpy as jnp
from jax import lax
from jax.experimental import pallas as pl
from jax.experimental.pallas import tpu as pltpu

F32 = jnp.float32
BF16 = jnp.bfloat16

DEPTH = 2
N_MIXERS = 2
CHUNK = 64
GLA_HEADS = 4
GLA_GATE_NORMALIZER = 16.0
GDN_QK_HEADS = 8
GDN_V_HEADS = 16
GDN_DK = 128
GDN_CONV = 4
XA_HEADS = 4
ALPHA = (2.0 * DEPTH) ** 0.25
LN_EPS = 1e-5
NORM_EPS = 1e-6

VMEM_LIMIT_BYTES = 56 * 1024 * 1024
LANES = 128
HALO = 8
FFN_SUB_ROWS = 512
PREP_QK_HEADS = 2


def _cparams(*sem):
    return pltpu.CompilerParams(dimension_semantics=sem, vmem_limit_bytes=VMEM_LIMIT_BYTES)


def _resident(shape):
    nd = len(shape)
    return pl.BlockSpec(shape, lambda *_: (0,) * nd, pipeline_mode=pl.Buffered(1))


def _dot(a, b):
    return jnp.dot(a, b, preferred_element_type=F32)


def _dot_nt(a, b):
    return lax.dot_general(a, b, (((1,), (1,)), ((), ())), preferred_element_type=F32)


def _dot_tn(a, b):
    return lax.dot_general(a, b, (((0,), (0,)), ((), ())), preferred_element_type=F32)


def _bmm(a, b):
    return jnp.einsum('nmk,nkp->nmp', a, b, preferred_element_type=F32)


def _bmm_nt(a, b):
    return jnp.einsum('nmk,npk->nmp', a, b, preferred_element_type=F32)


def _bmm_tn(a, b):
    return jnp.einsum('nkm,nkp->nmp', a, b, preferred_element_type=F32)


def _sigmoid(x):
    return jax.nn.sigmoid(x)


def _silu(x):
    return x * _sigmoid(x)


def _softplus(x):
    return jnp.maximum(x, 0.0) + jnp.log1p(jnp.exp(-jnp.abs(x)))


def _layer_norm(y, g, b):
    mu = jnp.mean(y, axis=-1, keepdims=True)
    yc = y - mu
    var = jnp.mean(yc * yc, axis=-1, keepdims=True)
    return yc * lax.rsqrt(var + LN_EPS) * g + b


def _chunk_cumsum(x, axis, reverse):
    n = x.shape[axis]
    pos = lax.broadcasted_iota(jnp.int32, x.shape, axis) % CHUNK
    s = 1
    while s < CHUNK:
        if reverse:
            shifted = pltpu.roll(x, n - s, axis)
            keep = pos < CHUNK - s
        else:
            shifted = pltpu.roll(x, s, axis)
            keep = pos >= s
        x = x + jnp.where(keep, shifted, 0.0)
        s *= 2
    return x


def _ffn_ln_kernel(x_ref, win_ref, wout_ref, g_ref, b_ref, o_ref, *, d_ff, n_split, sub):
    fc = d_ff // n_split
    for r0 in range(0, x_ref.shape[0], sub):
        x = x_ref[r0:r0 + sub, :]
        xb = x.astype(BF16)
        acc = jnp.zeros(x.shape, F32)
        for c in range(n_split):
            lo = c * fc
            gate = _dot(xb, win_ref[:, lo:lo + fc])
            up = _dot(xb, win_ref[:, d_ff + lo:d_ff + lo + fc])
            h = (_silu(gate) * up).astype(BF16)
            acc = acc + _dot(h, wout_ref[lo:lo + fc, :])
        y = ALPHA * x + 0.5 * acc
        o_ref[r0:r0 + sub, :] = _layer_norm(y, g_ref[...], b_ref[...])


def _ffn_ln(x, w_in, w_out, g, b, *, tm):
    t, d = x.shape
    d_ff = w_out.shape[0]
    tm = min(tm, t)
    return pl.pallas_call(
        functools.partial(_ffn_ln_kernel, d_ff=d_ff, n_split=2, sub=min(FFN_SUB_ROWS, tm)),
        grid=(t // tm,),
        in_specs=[
            pl.BlockSpec((tm, d), lambda i: (i, 0)),
            _resident(w_in.shape),
            _resident(w_out.shape),
            _resident(g.shape),
            _resident(b.shape),
        ],
        out_specs=pl.BlockSpec((tm, d), lambda i: (i, 0)),
        out_shape=jax.ShapeDtypeStruct((t, d), F32),
        compiler_params=_cparams("parallel"),
        name="ffn_ln",
    )(x, w_in, w_out, g, b)


def _proj_kernel(x_ref, w_ref, o_ref):
    o_ref[...] = _dot(x_ref[...].astype(BF16), w_ref[...]).astype(o_ref.dtype)


def _proj(x, w, *, tm, out_dtype):
    t, d = x.shape
    n = w.shape[1]
    tm = min(tm, t)
    return pl.pallas_call(
        _proj_kernel,
        grid=(t // tm,),
        in_specs=[pl.BlockSpec((tm, d), lambda i: (i, 0)), _resident(w.shape)],
        out_specs=pl.BlockSpec((tm, n), lambda i: (i, 0)),
        out_shape=jax.ShapeDtypeStruct((t, n), out_dtype),
        compiler_params=_cparams("parallel"),
        name="mem_kv_proj",
    )(x, w)


def _xattn_ln_kernel(x_ref, kv_ref, wq_ref, wo_ref, g_ref, b_ref, o_ref, *, heads):
    x = x_ref[0]
    d = x.shape[-1]
    dh = d // heads
    q = _dot(x.astype(BF16), wq_ref[...]).astype(BF16)
    outs = []
    for h in range(heads):
        qh = q[:, h * dh:(h + 1) * dh]
        kh = kv_ref[0, :, h * dh:(h + 1) * dh]
        vh = kv_ref[0, :, d + h * dh:d + (h + 1) * dh]
        s = _dot_nt(qh, kh) * (dh ** -0.5)
        e = jnp.exp(s - jnp.max(s, axis=-1, keepdims=True))
        den = jnp.sum(e, axis=-1, keepdims=True)
        outs.append(_dot(e.astype(BF16), vh) / den)
    o = jnp.concatenate(outs, axis=-1).astype(BF16)
    y = ALPHA * x + _dot(o, wo_ref[...])
    o_ref[0] = _layer_norm(y, g_ref[...], b_ref[...])


def _xattn_ln(x, kv, w_q, w_o, g, b, *, tm):
    bsz, s, d = x.shape
    n_mem = kv.shape[1]
    tm = min(tm, s)
    return pl.pallas_call(
        functools.partial(_xattn_ln_kernel, heads=XA_HEADS),
        grid=(bsz, s // tm),
        in_specs=[
            pl.BlockSpec((1, tm, d), lambda bi, i: (bi, i, 0)),
            pl.BlockSpec((1, n_mem, 2 * d), lambda bi, i: (bi, 0, 0)),
            _resident(w_q.shape),
            _resident(w_o.shape),
            _resident(g.shape),
            _resident(b.shape),
        ],
        out_specs=pl.BlockSpec((1, tm, d), lambda bi, i: (bi, i, 0)),
        out_shape=jax.ShapeDtypeStruct((bsz, s, d), F32),
        compiler_params=_cparams("parallel", "parallel"),
        name="xattn_ln",
    )(x, kv, w_q, w_o, g, b)


def _mixer_out_kernel(of_ref, or_ref, r_ref, x_ref, nw_ref, wo_ref, g_ref, b_ref, o_ref, *, dv, head_major):
    nw = nw_ref[...]
    parts = []
    for h in range(r_ref.shape[-1] // dv):
        if head_major:
            oh = of_ref[0, h].astype(F32) + or_ref[0, h].astype(F32)
        else:
            oh = (of_ref[0, :, h * dv:(h + 1) * dv].astype(F32)
                  + or_ref[0, :, h * dv:(h + 1) * dv].astype(F32))
        ms = jnp.mean(oh * oh, axis=-1, keepdims=True)
        parts.append(oh * lax.rsqrt(ms + NORM_EPS) * nw)
    gated = (jnp.concatenate(parts, axis=-1) * _silu(r_ref[0].astype(F32))).astype(BF16)
    y = ALPHA * x_ref[0] + _dot(gated, wo_ref[...])
    o_ref[0] = _layer_norm(y, g_ref[...], b_ref[...])


def _mixer_out(o_f, o_r, r, x, norm_w, w_out, g, b, *, tm):
    bsz, s, d = x.shape
    dh = r.shape[-1]
    dv = norm_w.shape[-1]
    tm = min(tm, s)
    head_major = o_f.ndim == 4
    tok = lambda n: pl.BlockSpec((1, tm, n), lambda bi, i: (bi, i, 0))
    o_spec = pl.BlockSpec((1, dh // dv, tm, dv), lambda bi, i: (bi, 0, i, 0)) if head_major else tok(dh)
    return pl.pallas_call(
        functools.partial(_mixer_out_kernel, dv=dv, head_major=head_major),
        grid=(bsz, s // tm),
        in_specs=[
            o_spec, o_spec, tok(dh), tok(d),
            _resident(norm_w.shape),
            _resident(w_out.shape),
            _resident(g.shape),
            _resident(b.shape),
        ],
        out_specs=tok(d),
        out_shape=jax.ShapeDtypeStruct((bsz, s, d), F32),
        compiler_params=_cparams("parallel", "parallel"),
        name="mixer_out_ln",
    )(o_f, o_r, r, x, norm_w, w_out, g, b)


def _gla_proj_kernel(x_ref, win_ref, wgd_ref, wgu_ref, bg_ref,
                     q_ref, k_ref, v_ref, r_ref, g_ref, *, hk, hv):
    xb = x_ref[...].astype(BF16)
    dk = hk // GLA_HEADS
    q_ref[...] = _dot(xb, win_ref[:, 0:hk]) * (dk ** -0.5)
    k_ref[...] = _dot(xb, win_ref[:, hk:2 * hk])
    v_ref[...] = _dot(xb, win_ref[:, 2 * hk:2 * hk + hv])
    r_ref[...] = _dot(xb, win_ref[:, 2 * hk + hv:2 * hk + 2 * hv]).astype(r_ref.dtype)
    low = _dot(xb, wgd_ref[...]).astype(BF16)
    logit = _dot(low, wgu_ref[...]) + bg_ref[...]
    logsig = jnp.minimum(logit, 0.0) - jnp.log1p(jnp.exp(-jnp.abs(logit)))
    gate = logsig / GLA_GATE_NORMALIZER
    g_ref[0] = gate[:, 0:hk]
    g_ref[1] = gate[:, hk:2 * hk]


def _gla_proj(x, w_in, w_gd, w_gu, b_g, *, tm):
    t, d = x.shape
    hk = w_gu.shape[1] // 2
    hv = (w_in.shape[1] - 2 * hk) // 2
    tm = min(tm, t)
    row = lambda n: pl.BlockSpec((tm, n), lambda i: (i, 0))
    return pl.pallas_call(
        functools.partial(_gla_proj_kernel, hk=hk, hv=hv),
        grid=(t // tm,),
        in_specs=[row(d), _resident(w_in.shape), _resident(w_gd.shape),
                  _resident(w_gu.shape), _resident(b_g.shape)],
        out_specs=[row(hk), row(hk), row(hv), row(hv),
                   pl.BlockSpec((2, tm, hk), lambda i: (0, i, 0))],
        out_shape=[
            jax.ShapeDtypeStruct((t, hk), F32),
            jax.ShapeDtypeStruct((t, hk), F32),
            jax.ShapeDtypeStruct((t, hv), F32),
            jax.ShapeDtypeStruct((t, hv), BF16),
            jax.ShapeDtypeStruct((2, t, hk), F32),
        ],
        compiler_params=_cparams("parallel"),
        name="gla_proj",
    )(x, w_in, w_gd, w_gu, b_g)


def _gla_scan_kernel(q_ref, k_ref, v_ref, g_ref, o_ref, st_ref, *, reverse):
    @pl.when(pl.program_id(1) == 0)
    def _():
        st_ref[...] = jnp.zeros(st_ref.shape, F32)

    heads, dv, dk = st_ref.shape
    cb = q_ref.shape[1]
    nc = cb // CHUNK

    def by_head(a, width):
        return jnp.concatenate([a[:, h * width:(h + 1) * width].reshape(nc, CHUNK, width)
                                for h in range(heads)], axis=0)

    b = by_head(_chunk_cumsum(g_ref[0, 0], 0, reverse), dk)
    q = by_head(q_ref[0], dk)
    k = by_head(k_ref[0], dk)
    vb = by_head(v_ref[0].astype(BF16), dv)
    mid = CHUNK // 2 if reverse else CHUNK // 2 - 1
    last = 0 if reverse else CHUNK - 1
    b_mid = b[:, mid:mid + 1, :]
    b_last = b[:, last:last + 1, :]

    qe = (q * jnp.exp(b - b_mid)).astype(BF16)
    ke = (k * jnp.exp(b_mid - b)).astype(BF16)
    scores = _bmm_nt(qe, ke)
    row = lax.broadcasted_iota(jnp.int32, (CHUNK, CHUNK), 0)
    col = lax.broadcasted_iota(jnp.int32, (CHUNK, CHUNK), 1)
    visible = (col >= row) if reverse else (col <= row)
    o = _bmm(jnp.where(visible, scores, 0.0).astype(BF16), vb)

    q_start = (q * jnp.exp(b)).astype(BF16)
    k_end = (k * jnp.exp(b_last - b)).astype(BF16)
    decay = jnp.exp(b_last)
    zt = _bmm_tn(vb, k_end)
    states = [None] * (heads * nc)
    finals = []
    for h in range(heads):
        st = st_ref[h]
        for i in (range(nc - 1, -1, -1) if reverse else range(nc)):
            states[h * nc + i] = st.astype(BF16)
            st = st * decay[h * nc + i] + zt[h * nc + i]
        finals.append(st)
    o = o + _bmm_nt(q_start, jnp.stack(states))
    for h in range(heads):
        st_ref[h] = finals[h]
        o_ref[0, :, h * dv:(h + 1) * dv] = o[h * nc:(h + 1) * nc].reshape(cb, dv).astype(o_ref.dtype)


def _gla_scan(q, k, v, g, *, direction, cb):
    bsz, s, hk = q.shape
    hv = v.shape[-1]
    dk, dv = hk // GLA_HEADS, hv // GLA_HEADS
    cb = min(cb, s)
    nb = s // cb
    reverse = direction == 1
    blk = (lambda n: nb - 1 - n) if reverse else (lambda n: n)
    tok = lambda n: pl.BlockSpec((1, cb, n), lambda bi, n_: (bi, blk(n_), 0))
    return pl.pallas_call(
        functools.partial(_gla_scan_kernel, reverse=reverse),
        grid=(bsz, nb),
        in_specs=[tok(hk), tok(hk), tok(hv),
                  pl.BlockSpec((1, 1, cb, hk), lambda bi, n: (direction, bi, blk(n), 0))],
        out_specs=tok(hv),
        out_shape=jax.ShapeDtypeStruct((bsz, s, hv), BF16),
        scratch_shapes=[pltpu.VMEM((GLA_HEADS, dv, dk), F32)],
        compiler_params=_cparams("parallel", "arbitrary"),
        name="gla_scan_rev" if reverse else "gla_scan_fwd",
    )(q, k, v, g)


def _gdn_proj_kernel(x_ref, xp_ref, xn_ref, win_ref, cw_ref, wab_ref,
                     alog_ref, dtb_ref, alogt_ref, dtbt_ref,
                     q_ref, k_ref, v_ref, z_ref, gc_ref, beta_ref, gct_ref, p_ref, *, n_qk, n_v):
    i = pl.program_id(1)
    tm = x_ref.shape[1]
    xb = x_ref[0].astype(BF16)
    keep_prev = (i > 0).astype(F32)
    keep_next = (i < pl.num_programs(1) - 1).astype(F32)
    xx = jnp.concatenate([(xp_ref[0] * keep_prev).astype(BF16), xb,
                          (xn_ref[0] * keep_next).astype(BF16)], axis=0)
    n_conv = 2 * n_qk + n_v
    z_ref[0] = _dot(xb, win_ref[:, n_conv:n_conv + n_v]).astype(z_ref.dtype)

    for j in range(n_conv // GDN_DK):
        cs = slice(j * GDN_DK, (j + 1) * GDN_DK)
        if j % 2 == 0:
            pg = _dot(xx, win_ref[:, j * GDN_DK:(j + 2) * GDN_DK])
            p_ref[j] = pg[:, :GDN_DK]
            p_ref[j + 1] = pg[:, GDN_DK:]
        y = jnp.zeros((tm, GDN_DK), F32)
        for tap in range(GDN_CONV):
            y = y + cw_ref[tap:tap + 1, cs] * p_ref[j, pl.ds(HALO - 2 + tap, tm), :]
        y = _silu(y)
        if j < 2 * GDN_QK_HEADS:
            y = y * lax.rsqrt(jnp.sum(y * y, axis=-1, keepdims=True) + NORM_EPS)
            if j < GDN_QK_HEADS:
                q_ref[0, j] = y * (GDN_DK ** -0.5)
            else:
                k_ref[0, j - GDN_QK_HEADS] = y
        else:
            v_ref[0, j - 2 * GDN_QK_HEADS] = y

    hv = GDN_V_HEADS
    ab = _dot(xb, wab_ref[...])
    abt = ab.T
    for d in range(2):
        a = ab[:, 2 * d * hv:(2 * d + 1) * hv]
        bt = ab[:, (2 * d + 1) * hv:(2 * d + 2) * hv]
        g = -jnp.exp(alog_ref[d:d + 1, :]) * _softplus(a + dtb_ref[d:d + 1, :])
        gc_ref[d, 0] = _chunk_cumsum(g, 0, d == 1)
        beta_ref[d, 0] = _sigmoid(bt)
        at = abt[2 * d * hv:(2 * d + 1) * hv, :]
        gt = -jnp.exp(alogt_ref[d]) * _softplus(at + dtbt_ref[d])
        gct = _chunk_cumsum(gt, 1, d == 1)
        for c in range(tm // CHUNK):
            gct_ref[d, 0, c] = gct[:, c * CHUNK:(c + 1) * CHUNK]


def _gdn_proj(x, w_in, conv_w, w_ab, a_log, dt_bias, a_log_t, dt_bias_t, *, tm):
    bsz, s, d = x.shape
    hv = GDN_V_HEADS
    n_qk = GDN_QK_HEADS * GDN_DK
    n_v = hv * GDN_DK
    tm = min(tm, s)
    nt = s // tm
    hb = tm // HALO
    tok = lambda n: pl.BlockSpec((1, tm, n), lambda bi, i: (bi, i, 0))
    head = lambda n: pl.BlockSpec((1, n, tm, GDN_DK), lambda bi, i: (bi, 0, i, 0))
    gate = pl.BlockSpec((2, 1, tm, hv), lambda bi, i: (0, bi, i, 0))
    return pl.pallas_call(
        functools.partial(_gdn_proj_kernel, n_qk=n_qk, n_v=n_v),
        grid=(bsz, nt),
        in_specs=[
            tok(d),
            pl.BlockSpec((1, HALO, d), lambda bi, i: (bi, jnp.maximum(i * hb - 1, 0), 0)),
            pl.BlockSpec((1, HALO, d), lambda bi, i: (bi, jnp.minimum((i + 1) * hb, s // HALO - 1), 0)),
            _resident(w_in.shape), _resident(conv_w.shape), _resident(w_ab.shape),
            _resident(a_log.shape), _resident(dt_bias.shape),
            _resident(a_log_t.shape), _resident(dt_bias_t.shape),
        ],
        out_specs=[head(GDN_QK_HEADS), head(GDN_QK_HEADS), head(hv), tok(n_v), gate, gate,
                   pl.BlockSpec((2, 1, tm // CHUNK, hv, CHUNK), lambda bi, i: (0, bi, i, 0, 0))],
        out_shape=[
            jax.ShapeDtypeStruct((bsz, GDN_QK_HEADS, s, GDN_DK), F32),
            jax.ShapeDtypeStruct((bsz, GDN_QK_HEADS, s, GDN_DK), F32),
            jax.ShapeDtypeStruct((bsz, hv, s, GDN_DK), F32),
            jax.ShapeDtypeStruct((bsz, s, n_v), BF16),
            jax.ShapeDtypeStruct((2, bsz, s, hv), F32),
            jax.ShapeDtypeStruct((2, bsz, s, hv), F32),
            jax.ShapeDtypeStruct((2, bsz, s // CHUNK, hv, CHUNK), F32),
        ],
        scratch_shapes=[pltpu.VMEM(((2 * n_qk + n_v) // GDN_DK, tm + 2 * HALO, GDN_DK), F32)],
        compiler_params=_cparams("parallel", "parallel"),
        name="gdn_proj",
    )(x, x, x, w_in, conv_w, w_ab, a_log, dt_bias, a_log_t, dt_bias_t)


def _bmm3(a, b):
    return _bmm(a.astype(BF16), b.astype(BF16))


def _unit_tri_inverse(a):
    row = lax.broadcasted_iota(jnp.int32, (CHUNK, CHUNK), 0)
    col = lax.broadcasted_iota(jnp.int32, (CHUNK, CHUNK), 1)
    same = lambda s: (row // s) == (col // s)
    eye = (row == col).astype(F32)
    ad = jnp.where(same(8), a, 0.0)
    a2 = _bmm3(ad, ad)
    a4 = _bmm3(a2, a2)
    p1 = eye - ad + a2 - _bmm3(ad, a2)
    t = p1 + _bmm3(p1, a4)
    s = 8
    while s < CHUNK:
        off = jnp.logical_and(same(2 * s), jnp.logical_not(same(s)))
        x = _bmm3(jnp.where(off, a, 0.0), t)
        t = t - _bmm3(t, x)
        s *= 2
    return t


def _gdn_scan_kernel(q_ref, k_ref, v_ref, gc_ref, beta_ref, gct_ref, o_ref,
                     st_ref, u_ref, wq_ref, attn_ref, ke_ref, cdec_ref, *, reverse):
    @pl.when(pl.program_id(1) == 0)
    def _():
        st_ref[...] = jnp.zeros(st_ref.shape, F32)

    qk_heads, cb, dk = q_ref.shape[1:]
    hv = v_ref.shape[1]
    rep = hv // qk_heads
    nc = cb // CHUNK
    row = lax.broadcasted_iota(jnp.int32, (CHUNK, CHUNK), 0)
    col = lax.broadcasted_iota(jnp.int32, (CHUNK, CHUNK), 1)
    incl = (col >= row) if reverse else (col <= row)
    strict = (col > row) if reverse else (col < row)
    last = 0 if reverse else CHUNK - 1
    lane = lax.broadcasted_iota(jnp.int32, gc_ref.shape[2:], 1)

    def prep(it, carry):
        gc_all = gc_ref[0, 0]
        beta_all = beta_ref[0, 0]
        heads, a_list, rhs_list, late = [], [], [], []
        for pp in range(PREP_QK_HEADS):
            p = it * PREP_QK_HEADS + pp
            q = q_ref[0, p].reshape(nc, CHUNK, dk)
            k = k_ref[0, p].reshape(nc, CHUNK, dk)
            kb = k.astype(BF16)
            kk = _bmm_nt(kb, kb)
            qk = _bmm_nt(q.astype(BF16), kb)
            for hh in range(rep):
                j = p * rep + hh
                pick = lane == j
                gcol = jnp.sum(jnp.where(pick, gc_all, 0.0), axis=1, keepdims=True).reshape(nc, CHUNK, 1)
                bcol = jnp.sum(jnp.where(pick, beta_all, 0.0), axis=1, keepdims=True).reshape(nc, CHUNK, 1)
                grow = gct_ref[0, 0, :, pl.ds(j, 1), :]
                decay = jnp.where(incl, jnp.exp(jnp.where(incl, gcol - grow, 0.0)), 0.0)
                eg = jnp.exp(gcol)
                glast = gcol[:, last:last + 1, :]
                v = v_ref[0, j].reshape(nc, CHUNK, dk)
                heads.append(j)
                a_list.append(jnp.where(strict, bcol * kk * decay, 0.0))
                rhs_list.append(jnp.concatenate([v * bcol, k * (bcol * eg)], axis=-1).astype(BF16))
                late.append(((q * eg).astype(BF16), (qk * decay).astype(BF16),
                             (k * jnp.exp(glast - gcol)).astype(BF16),
                             jnp.broadcast_to(jnp.exp(glast), (nc, 1, dk))))
        t = _unit_tri_inverse(jnp.concatenate(a_list, axis=0)).astype(BF16)
        uw = _bmm(t, jnp.concatenate(rhs_list, axis=0))
        for n, (j, (q_start, attn, k_end, cdec)) in enumerate(zip(heads, late)):
            uw_j = uw[n * nc:(n + 1) * nc]
            u_ref[:, j] = uw_j[:, :, :dk]
            wq_ref[:, j] = jnp.concatenate([uw_j[:, :, dk:].astype(BF16), q_start], axis=1)
            attn_ref[:, j] = attn
            ke_ref[:, j] = k_end
            cdec_ref[:, j] = cdec
        return carry

    lax.fori_loop(0, qk_heads // PREP_QK_HEADS, prep, 0)

    def step(ii, carry):
        i = nc - 1 - ii if reverse else ii
        r0 = pl.multiple_of(i * CHUNK, CHUNK)
        st = st_ref[...]
        ws = _bmm(wq_ref[i], st.astype(BF16))
        v_new = (u_ref[i] - ws[:, :CHUNK]).astype(BF16)
        o_ref[0, :, pl.ds(r0, CHUNK), :] = (ws[:, CHUNK:] + _bmm(attn_ref[i], v_new)).astype(o_ref.dtype)
        st_ref[...] = st * cdec_ref[i] + _bmm_tn(ke_ref[i], v_new)
        return carry

    lax.fori_loop(0, nc, step, 0)


def _gdn_scan(q, k, v, gc, beta, gct, *, direction, cb):
    bsz, qk_heads, s, dk = q.shape
    hv = v.shape[1]
    cb = min(cb, s)
    nb = s // cb
    nc = cb // CHUNK
    reverse = direction == 1
    blk = (lambda n: nb - 1 - n) if reverse else (lambda n: n)
    head_major = lambda h: pl.BlockSpec((1, h, cb, dk), lambda bi, n: (bi, 0, blk(n), 0))
    gate = pl.BlockSpec((1, 1, cb, hv), lambda bi, n: (direction, bi, blk(n), 0))
    return pl.pallas_call(
        functools.partial(_gdn_scan_kernel, reverse=reverse),
        grid=(bsz, nb),
        in_specs=[
            head_major(qk_heads), head_major(qk_heads), head_major(hv), gate, gate,
            pl.BlockSpec((1, 1, nc, hv, CHUNK), lambda bi, n: (direction, bi, blk(n), 0, 0)),
        ],
        out_specs=head_major(hv),
        out_shape=jax.ShapeDtypeStruct((bsz, hv, s, dk), BF16),
        scratch_shapes=[
            pltpu.VMEM((hv, dk, dk), F32),
            pltpu.VMEM((nc, hv, CHUNK, dk), F32),
            pltpu.VMEM((nc, hv, 2 * CHUNK, dk), BF16),
            pltpu.VMEM((nc, hv, CHUNK, CHUNK), BF16),
            pltpu.VMEM((nc, hv, CHUNK, dk), BF16),
            pltpu.VMEM((nc, hv, 1, dk), F32),
        ],
        compiler_params=_cparams("parallel", "arbitrary"),
        name="gdn_scan_rev" if reverse else "gdn_scan_fwd",
    )(q, k, v, gc, beta, gct)


def _prepare_weights(ffn_w_in, ffn_w_out, ln_g, ln_b,
                     gla_w_in, gla_w_gate_down, gla_w_gate_up, gla_b_gate, gla_norm_w, gla_w_out,
                     gdn_w_in, gdn_conv_w, gdn_w_ab, gdn_a_log, gdn_dt_bias, gdn_norm_w, gdn_w_out,
                     xa_w_q, xa_w_kv, xa_w_o):
    n_gla, _, d, rank = gla_w_gate_down.shape
    hk = gla_w_gate_up.shape[-1]
    w_gd = jnp.transpose(gla_w_gate_down, (0, 2, 1, 3)).reshape(n_gla, d, 2 * rank)
    zeros = jnp.zeros((n_gla, rank, hk), F32)
    w_gu = jnp.concatenate([
        jnp.concatenate([gla_w_gate_up[:, 0], zeros], axis=-1),
        jnp.concatenate([zeros, gla_w_gate_up[:, 1]], axis=-1)], axis=1)
    n_gdn = gdn_w_ab.shape[0]
    w_ab = jnp.transpose(gdn_w_ab, (0, 2, 1, 3)).reshape(n_gdn, d, -1)
    return dict(
        ffn_w_in=ffn_w_in.astype(BF16), ffn_w_out=ffn_w_out.astype(BF16),
        ln_g=ln_g[:, :, None, :], ln_b=ln_b[:, :, None, :],
        gla_w_in=gla_w_in.astype(BF16), gla_w_gd=w_gd.astype(BF16), gla_w_gu=w_gu.astype(BF16),
        gla_b_g=gla_b_gate.reshape(n_gla, 1, 2 * hk), gla_norm_w=gla_norm_w[:, None, :],
        gla_w_out=gla_w_out.astype(BF16),
        gdn_w_in=gdn_w_in.astype(BF16), gdn_conv_w=gdn_conv_w,
        gdn_w_ab=jnp.pad(w_ab, ((0, 0), (0, 0), (0, LANES - w_ab.shape[-1]))).astype(BF16),
        gdn_a_log=gdn_a_log, gdn_dt_bias=gdn_dt_bias,
        gdn_a_log_t=gdn_a_log[..., None], gdn_dt_bias_t=gdn_dt_bias[..., None],
        gdn_norm_w=gdn_norm_w[:, None, :], gdn_w_out=gdn_w_out.astype(BF16),
        xa_w_q=xa_w_q.astype(BF16), xa_w_kv=xa_w_kv.astype(BF16), xa_w_o=xa_w_o.astype(BF16),
    )


def _trunk(x, mem, w):
    bsz, s, d = x.shape
    t = bsz * s
    tm = 512
    tm_ffn = 2 * FFN_SUB_ROWS
    cb = 512
    for i in range(DEPTH):
        ln = lambda n: (w['ln_g'][i, n], w['ln_b'][i, n])
        x2 = _ffn_ln(x.reshape(t, d), w['ffn_w_in'][i, 0], w['ffn_w_out'][i, 0], *ln(0), tm=tm_ffn)
        j = i // N_MIXERS
        if i % N_MIXERS == 0:
            q, k, v, r, g = _gla_proj(x2, w['gla_w_in'][j], w['gla_w_gd'][j], w['gla_w_gu'][j],
                                      w['gla_b_g'][j], tm=tm)
            q, k, v, r = (a.reshape(bsz, s, -1) for a in (q, k, v, r))
            g = g.reshape(2, bsz, s, -1)
            o_f = _gla_scan(q, k, v, g, direction=0, cb=cb)
            o_r = _gla_scan(q, k, v, g, direction=1, cb=cb)
            norm_w, w_out = w['gla_norm_w'][j], w['gla_w_out'][j]
        else:
            q, k, v, r, gc, beta, gct = _gdn_proj(
                x2.reshape(bsz, s, d), w['gdn_w_in'][j], w['gdn_conv_w'][j], w['gdn_w_ab'][j],
                w['gdn_a_log'][j], w['gdn_dt_bias'][j],
                w['gdn_a_log_t'][j], w['gdn_dt_bias_t'][j], tm=tm)
            o_f = _gdn_scan(q, k, v, gc, beta, gct, direction=0, cb=cb)
            o_r = _gdn_scan(q, k, v, gc, beta, gct, direction=1, cb=cb)
            norm_w, w_out = w['gdn_norm_w'][j], w['gdn_w_out'][j]
        x2 = _mixer_out(o_f, o_r, r, x2.reshape(bsz, s, d), norm_w, w_out, *ln(1), tm=tm)
        kv = _proj(mem.reshape(-1, d), w['xa_w_kv'][i], tm=256, out_dtype=BF16)
        x3 = _xattn_ln(x2, kv.reshape(bsz, -1, 2 * d),
                       w['xa_w_q'][i], w['xa_w_o'][i], *ln(2), tm=tm)
        x = _ffn_ln(x3.reshape(t, d), w['ffn_w_in'][i, 1], w['ffn_w_out'][i, 1], *ln(3), tm=tm_ffn)
        x = x.reshape(bsz, s, d)
    return x


def kernel(x_prompt, x_sample, mem_prompt, mem_sample, ffn_w_in, ffn_w_out, ln_g, ln_b, gla_w_in, gla_w_gate_down, gla_w_gate_up, gla_b_gate, gla_norm_w, gla_w_out, gdn_w_in, gdn_conv_w, gdn_w_ab, gdn_a_log, gdn_dt_bias, gdn_norm_w, gdn_w_out, xa_w_q, xa_w_kv, xa_w_o):
    w = _prepare_weights(ffn_w_in, ffn_w_out, ln_g, ln_b,
                         gla_w_in, gla_w_gate_down, gla_w_gate_up, gla_b_gate, gla_norm_w, gla_w_out,
                         gdn_w_in, gdn_conv_w, gdn_w_ab, gdn_a_log, gdn_dt_bias, gdn_norm_w, gdn_w_out,
                         xa_w_q, xa_w_kv, xa_w_o)
    return (_trunk(x_prompt, mem_prompt, w), _trunk(x_sample, mem_sample, w))
```

```python
import functools

import jax
import jax.numpy as jnp
from jax import lax
from jax.experimental import pallas as pl
from jax.experimental.pallas import tpu as pltpu

F32 = jnp.float32
BF16 = jnp.bfloat16

DEPTH = 2
N_MIXERS = 2
CHUNK = 64
GLA_HEADS = 4
GLA_GATE_NORMALIZER = 16.0
GDN_QK_HEADS = 8
GDN_V_HEADS = 16
GDN_DK = 128
GDN_CONV = 4
XA_HEADS = 4
ALPHA = (2.0 * DEPTH) ** 0.25
LN_EPS = 1e-5
NORM_EPS = 1e-6

VMEM_LIMIT_BYTES = 56 * 1024 * 1024
LANES = 128
MXU_DIM = 256
HALO = 8
SUB_ROWS = 512
PREP_QK_HEADS = 2


def _cparams(*sem):
    return pltpu.CompilerParams(dimension_semantics=sem, vmem_limit_bytes=VMEM_LIMIT_BYTES)


def _resident(shape):
    nd = len(shape)
    return pl.BlockSpec(shape, lambda *_: (0,) * nd, pipeline_mode=pl.Buffered(1))


def _dot(a, b):
    return jnp.dot(a, b, preferred_element_type=F32)


def _dot_nt(a, b):
    return lax.dot_general(a, b, (((1,), (1,)), ((), ())), preferred_element_type=F32)


def _dot_tn(a, b):
    return lax.dot_general(a, b, (((0,), (0,)), ((), ())), preferred_element_type=F32)


def _bmm(a, b):
    return jnp.einsum('nmk,nkp->nmp', a, b, preferred_element_type=F32)


def _bmm_nt(a, b):
    return jnp.einsum('nmk,npk->nmp', a, b, preferred_element_type=F32)


def _bmm_tn(a, b):
    return jnp.einsum('nkm,nkp->nmp', a, b, preferred_element_type=F32)


def _sigmoid(x):
    return jax.nn.sigmoid(x)


def _silu(x):
    return x * _sigmoid(x)


def _softplus(x):
    return jnp.maximum(x, 0.0) + jnp.log1p(jnp.exp(-jnp.abs(x)))


def _layer_norm(y, g, b):
    mu = jnp.mean(y, axis=-1, keepdims=True)
    yc = y - mu
    var = jnp.mean(yc * yc, axis=-1, keepdims=True)
    return yc * lax.rsqrt(var + LN_EPS) * g + b


def _chunk_cumsum(x, axis, reverse):
    n = x.shape[axis]
    pos = lax.broadcasted_iota(jnp.int32, x.shape, axis) % CHUNK
    s = 1
    while s < CHUNK:
        if reverse:
            shifted = pltpu.roll(x, n - s, axis)
            keep = pos < CHUNK - s
        else:
            shifted = pltpu.roll(x, s, axis)
            keep = pos >= s
        x = x + jnp.where(keep, shifted, 0.0)
        s *= 2
    return x


def _ffn_ln_kernel(x_ref, win_ref, wout_ref, g_ref, b_ref, o_ref, *, d_ff, n_split, sub):
    tiles = d_ff // MXU_DIM
    cuts = [MXU_DIM * (tiles * c // n_split) for c in range(n_split)] + [d_ff]
    for r0 in range(0, x_ref.shape[0], sub):
        x = x_ref[r0:r0 + sub, :]
        xb = x.astype(BF16)
        acc = jnp.zeros(x.shape, F32)
        for lo, hi in zip(cuts[:-1], cuts[1:]):
            gate = _dot(xb, win_ref[:, lo:hi])
            up = _dot(xb, win_ref[:, d_ff + lo:d_ff + hi])
            h = (_silu(gate) * up).astype(BF16)
            acc = acc + _dot(h, wout_ref[lo:hi, :])
        y = ALPHA * x + 0.5 * acc
        o_ref[r0:r0 + sub, :] = _layer_norm(y, g_ref[...], b_ref[...])


def _ffn_ln(x, w_in, w_out, g, b, *, tm):
    t, d = x.shape
    d_ff = w_out.shape[0]
    tm = min(tm, t)
    return pl.pallas_call(
        functools.partial(_ffn_ln_kernel, d_ff=d_ff, n_split=2, sub=min(SUB_ROWS, tm)),
        grid=(t // tm,),
        in_specs=[
            pl.BlockSpec((tm, d), lambda i: (i, 0)),
            _resident(w_in.shape),
            _resident(w_out.shape),
            _resident(g.shape),
            _resident(b.shape),
        ],
        out_specs=pl.BlockSpec((tm, d), lambda i: (i, 0)),
        out_shape=jax.ShapeDtypeStruct((t, d), F32),
        compiler_params=_cparams("parallel"),
        name="ffn_ln",
    )(x, w_in, w_out, g, b)


def _proj_kernel(x_ref, w_ref, o_ref):
    o_ref[...] = _dot(x_ref[...].astype(BF16), w_ref[...]).astype(o_ref.dtype)


def _proj(x, w, *, tm, out_dtype):
    t, d = x.shape
    n = w.shape[1]
    tm = min(tm, t)
    return pl.pallas_call(
        _proj_kernel,
        grid=(t // tm,),
        in_specs=[pl.BlockSpec((tm, d), lambda i: (i, 0)), _resident(w.shape)],
        out_specs=pl.BlockSpec((tm, n), lambda i: (i, 0)),
        out_shape=jax.ShapeDtypeStruct((t, n), out_dtype),
        compiler_params=_cparams("parallel"),
        name="mem_kv_proj",
    )(x, w)


def _xattn_ln_kernel(x_ref, kv_ref, wq_ref, wo_ref, g_ref, b_ref, o_ref, *, heads, sub):
    d = x_ref.shape[-1]
    dh = d // heads
    for r0 in range(0, x_ref.shape[1], sub):
        x = x_ref[0, r0:r0 + sub, :]
        q = _dot(x.astype(BF16), wq_ref[...]).astype(BF16)
        outs = []
        for h in range(heads):
            qh = q[:, h * dh:(h + 1) * dh]
            kh = kv_ref[0, :, h * dh:(h + 1) * dh]
            vh = kv_ref[0, :, d + h * dh:d + (h + 1) * dh]
            s = _dot_nt(qh, kh) * (dh ** -0.5)
            e = jnp.exp(s - jnp.max(s, axis=-1, keepdims=True))
            den = jnp.sum(e, axis=-1, keepdims=True)
            outs.append(_dot(e.astype(BF16), vh) / den)
        o = jnp.concatenate(outs, axis=-1).astype(BF16)
        y = ALPHA * x + _dot(o, wo_ref[...])
        o_ref[0, r0:r0 + sub, :] = _layer_norm(y, g_ref[...], b_ref[...])


def _xattn_ln(x, kv, w_q, w_o, g, b, *, tm):
    bsz, s, d = x.shape
    n_mem = kv.shape[1]
    tm = min(tm, s)
    return pl.pallas_call(
        functools.partial(_xattn_ln_kernel, heads=XA_HEADS, sub=min(SUB_ROWS, tm)),
        grid=(bsz, s // tm),
        in_specs=[
            pl.BlockSpec((1, tm, d), lambda bi, i: (bi, i, 0)),
            pl.BlockSpec((1, n_mem, 2 * d), lambda bi, i: (bi, 0, 0)),
            _resident(w_q.shape),
            _resident(w_o.shape),
            _resident(g.shape),
            _resident(b.shape),
        ],
        out_specs=pl.BlockSpec((1, tm, d), lambda bi, i: (bi, i, 0)),
        out_shape=jax.ShapeDtypeStruct((bsz, s, d), F32),
        compiler_params=_cparams("parallel", "parallel"),
        name="xattn_ln",
    )(x, kv, w_q, w_o, g, b)


def _mixer_out_kernel(of_ref, or_ref, r_ref, x_ref, nw_ref, wo_ref, g_ref, b_ref, o_ref, *, dv, head_major, sub):
    nw = nw_ref[...]
    for r0 in range(0, x_ref.shape[1], sub):
        rows = slice(r0, r0 + sub)
        parts = []
        for h in range(r_ref.shape[-1] // dv):
            if head_major:
                oh = of_ref[0, h, rows, :].astype(F32) + or_ref[0, h, rows, :].astype(F32)
            else:
                oh = (of_ref[0, rows, h * dv:(h + 1) * dv].astype(F32)
                      + or_ref[0, rows, h * dv:(h + 1) * dv].astype(F32))
            ms = jnp.mean(oh * oh, axis=-1, keepdims=True)
            parts.append(oh * lax.rsqrt(ms + NORM_EPS) * nw)
        gated = (jnp.concatenate(parts, axis=-1) * _silu(r_ref[0, rows, :].astype(F32))).astype(BF16)
        y = ALPHA * x_ref[0, rows, :] + _dot(gated, wo_ref[...])
        o_ref[0, rows, :] = _layer_norm(y, g_ref[...], b_ref[...])


def _mixer_out(o_f, o_r, r, x, norm_w, w_out, g, b, *, tm):
    bsz, s, d = x.shape
    dh = r.shape[-1]
    dv = norm_w.shape[-1]
    tm = min(tm, s)
    head_major = o_f.ndim == 4
    tok = lambda n: pl.BlockSpec((1, tm, n), lambda bi, i: (bi, i, 0))
    o_spec = pl.BlockSpec((1, dh // dv, tm, dv), lambda bi, i: (bi, 0, i, 0)) if head_major else tok(dh)
    return pl.pallas_call(
        functools.partial(_mixer_out_kernel, dv=dv, head_major=head_major, sub=min(SUB_ROWS, tm)),
        grid=(bsz, s // tm),
        in_specs=[
            o_spec, o_spec, tok(dh), tok(d),
            _resident(norm_w.shape),
            _resident(w_out.shape),
            _resident(g.shape),
            _resident(b.shape),
        ],
        out_specs=tok(d),
        out_shape=jax.ShapeDtypeStruct((bsz, s, d), F32),
        compiler_params=_cparams("parallel", "parallel"),
        name="mixer_out_ln",
    )(o_f, o_r, r, x, norm_w, w_out, g, b)


def _gla_proj_kernel(x_ref, win_ref, wgd_ref, wgu_ref, bg_ref,
                     q_ref, k_ref, v_ref, r_ref, g_ref, *, hk, hv, sub):
    dk = hk // GLA_HEADS
    for r0 in range(0, x_ref.shape[0], sub):
        rows = slice(r0, r0 + sub)
        xb = x_ref[rows, :].astype(BF16)
        q_ref[rows, :] = _dot(xb, win_ref[:, 0:hk]) * (dk ** -0.5)
        k_ref[rows, :] = _dot(xb, win_ref[:, hk:2 * hk])
        v_ref[rows, :] = _dot(xb, win_ref[:, 2 * hk:2 * hk + hv])
        r_ref[rows, :] = _dot(xb, win_ref[:, 2 * hk + hv:2 * hk + 2 * hv]).astype(r_ref.dtype)
        low = _dot(xb, wgd_ref[...]).astype(BF16)
        logit = _dot(low, wgu_ref[...]) + bg_ref[...]
        logsig = jnp.minimum(logit, 0.0) - jnp.log1p(jnp.exp(-jnp.abs(logit)))
        gate = logsig / GLA_GATE_NORMALIZER
        g_ref[0, rows, :] = gate[:, 0:hk]
        g_ref[1, rows, :] = gate[:, hk:2 * hk]


def _gla_proj(x, w_in, w_gd, w_gu, b_g, *, tm):
    t, d = x.shape
    hk = w_gu.shape[1] // 2
    hv = (w_in.shape[1] - 2 * hk) // 2
    tm = min(tm, t)
    row = lambda n: pl.BlockSpec((tm, n), lambda i: (i, 0))
    return pl.pallas_call(
        functools.partial(_gla_proj_kernel, hk=hk, hv=hv, sub=min(SUB_ROWS, tm)),
        grid=(t // tm,),
        in_specs=[row(d), _resident(w_in.shape), _resident(w_gd.shape),
                  _resident(w_gu.shape), _resident(b_g.shape)],
        out_specs=[row(hk), row(hk), row(hv), row(hv),
                   pl.BlockSpec((2, tm, hk), lambda i: (0, i, 0))],
        out_shape=[
            jax.ShapeDtypeStruct((t, hk), F32),
            jax.ShapeDtypeStruct((t, hk), F32),
            jax.ShapeDtypeStruct((t, hv), F32),
            jax.ShapeDtypeStruct((t, hv), BF16),
            jax.ShapeDtypeStruct((2, t, hk), F32),
        ],
        compiler_params=_cparams("parallel"),
        name="gla_proj",
    )(x, w_in, w_gd, w_gu, b_g)


def _gla_scan_kernel(q_ref, k_ref, v_ref, g_ref, o_ref, st_ref, *, reverse):
    @pl.when(pl.program_id(1) == 0)
    def _():
        st_ref[...] = jnp.zeros(st_ref.shape, F32)

    heads, dv, dk = st_ref.shape
    cb = q_ref.shape[1]
    nc = cb // CHUNK

    def by_head(a, width):
        return jnp.concatenate([a[:, h * width:(h + 1) * width].reshape(nc, CHUNK, width)
                                for h in range(heads)], axis=0)

    b = by_head(_chunk_cumsum(g_ref[0, 0], 0, reverse), dk)
    q = by_head(q_ref[0], dk)
    k = by_head(k_ref[0], dk)
    vb = by_head(v_ref[0].astype(BF16), dv)
    mid = CHUNK // 2 if reverse else CHUNK // 2 - 1
    last = 0 if reverse else CHUNK - 1
    b_mid = b[:, mid:mid + 1, :]
    b_last = b[:, last:last + 1, :]

    qe = (q * jnp.exp(b - b_mid)).astype(BF16)
    ke = (k * jnp.exp(b_mid - b)).astype(BF16)
    scores = _bmm_nt(qe, ke)
    row = lax.broadcasted_iota(jnp.int32, (CHUNK, CHUNK), 0)
    col = lax.broadcasted_iota(jnp.int32, (CHUNK, CHUNK), 1)
    visible = (col >= row) if reverse else (col <= row)
    o = _bmm(jnp.where(visible, scores, 0.0).astype(BF16), vb)

    q_start = (q * jnp.exp(b)).astype(BF16)
    k_end = (k * jnp.exp(b_last - b)).astype(BF16)
    decay = jnp.exp(b_last)
    zt = _bmm_tn(vb, k_end)
    states = [None] * (heads * nc)
    finals = []
    for h in range(heads):
        st = st_ref[h]
        for i in (range(nc - 1, -1, -1) if reverse else range(nc)):
            states[h * nc + i] = st.astype(BF16)
            st = st * decay[h * nc + i] + zt[h * nc + i]
        finals.append(st)
    o = o + _bmm_nt(q_start, jnp.stack(states))
    for h in range(heads):
        st_ref[h] = finals[h]
        o_ref[0, :, h * dv:(h + 1) * dv] = o[h * nc:(h + 1) * nc].reshape(cb, dv).astype(o_ref.dtype)


def _gla_scan(q, k, v, g, *, direction, cb):
    bsz, s, hk = q.shape
    hv = v.shape[-1]
    dk, dv = hk // GLA_HEADS, hv // GLA_HEADS
    cb = min(cb, s)
    nb = s // cb
    reverse = direction == 1
    blk = (lambda n: nb - 1 - n) if reverse else (lambda n: n)
    tok = lambda n: pl.BlockSpec((1, cb, n), lambda bi, n_: (bi, blk(n_), 0))
    return pl.pallas_call(
        functools.partial(_gla_scan_kernel, reverse=reverse),
        grid=(bsz, nb),
        in_specs=[tok(hk), tok(hk), tok(hv),
                  pl.BlockSpec((1, 1, cb, hk), lambda bi, n: (direction, bi, blk(n), 0))],
        out_specs=tok(hv),
        out_shape=jax.ShapeDtypeStruct((bsz, s, hv), BF16),
        scratch_shapes=[pltpu.VMEM((GLA_HEADS, dv, dk), F32)],
        compiler_params=_cparams("parallel", "arbitrary"),
        name="gla_scan_rev" if reverse else "gla_scan_fwd",
    )(q, k, v, g)


def _gdn_proj_kernel(x_ref, xp_ref, xn_ref, win_ref, cw_ref, wab_ref,
                     alog_ref, dtb_ref, alogt_ref, dtbt_ref,
                     q_ref, k_ref, v_ref, z_ref, gc_ref, beta_ref, gct_ref, p_ref, *, n_qk, n_v):
    i = pl.program_id(1)
    tm = x_ref.shape[1]
    xb = x_ref[0].astype(BF16)
    keep_prev = (i > 0).astype(F32)
    keep_next = (i < pl.num_programs(1) - 1).astype(F32)
    xx = jnp.concatenate([(xp_ref[0] * keep_prev).astype(BF16), xb,
                          (xn_ref[0] * keep_next).astype(BF16)], axis=0)
    n_conv = 2 * n_qk + n_v
    z_ref[0] = _dot(xb, win_ref[:, n_conv:n_conv + n_v]).astype(z_ref.dtype)

    for j in range(n_conv // GDN_DK):
        cs = slice(j * GDN_DK, (j + 1) * GDN_DK)
        if j % 2 == 0:
            pg = _dot(xx, win_ref[:, j * GDN_DK:(j + 2) * GDN_DK])
            p_ref[j] = pg[:, :GDN_DK]
            p_ref[j + 1] = pg[:, GDN_DK:]
        y = jnp.zeros((tm, GDN_DK), F32)
        for tap in range(GDN_CONV):
            y = y + cw_ref[tap:tap + 1, cs] * p_ref[j, pl.ds(HALO - 2 + tap, tm), :]
        y = _silu(y)
        if j < 2 * GDN_QK_HEADS:
            y = y * lax.rsqrt(jnp.sum(y * y, axis=-1, keepdims=True) + NORM_EPS)
            if j < GDN_QK_HEADS:
                q_ref[0, j] = y * (GDN_DK ** -0.5)
            else:
                k_ref[0, j - GDN_QK_HEADS] = y
        else:
            v_ref[0, j - 2 * GDN_QK_HEADS] = y

    hv = GDN_V_HEADS
    ab = _dot(xb, wab_ref[...])
    abt = ab.T
    for d in range(2):
        a = ab[:, 2 * d * hv:(2 * d + 1) * hv]
        bt = ab[:, (2 * d + 1) * hv:(2 * d + 2) * hv]
        g = -jnp.exp(alog_ref[d:d + 1, :]) * _softplus(a + dtb_ref[d:d + 1, :])
        gc_ref[d, 0] = _chunk_cumsum(g, 0, d == 1)
        beta_ref[d, 0] = _sigmoid(bt)
        at = abt[2 * d * hv:(2 * d + 1) * hv, :]
        gt = -jnp.exp(alogt_ref[d]) * _softplus(at + dtbt_ref[d])
        gct = _chunk_cumsum(gt, 1, d == 1)
        for c in range(tm // CHUNK):
            gct_ref[d, 0, c] = gct[:, c * CHUNK:(c + 1) * CHUNK]


def _gdn_proj(x, w_in, conv_w, w_ab, a_log, dt_bias, a_log_t, dt_bias_t, *, tm):
    bsz, s, d = x.shape
    hv = GDN_V_HEADS
    n_qk = GDN_QK_HEADS * GDN_DK
    n_v = hv * GDN_DK
    tm = min(tm, s)
    nt = s // tm
    hb = tm // HALO
    tok = lambda n: pl.BlockSpec((1, tm, n), lambda bi, i: (bi, i, 0))
    head = lambda n: pl.BlockSpec((1, n, tm, GDN_DK), lambda bi, i: (bi, 0, i, 0))
    gate = pl.BlockSpec((2, 1, tm, hv), lambda bi, i: (0, bi, i, 0))
    return pl.pallas_call(
        functools.partial(_gdn_proj_kernel, n_qk=n_qk, n_v=n_v),
        grid=(bsz, nt),
        in_specs=[
            tok(d),
            pl.BlockSpec((1, HALO, d), lambda bi, i: (bi, jnp.maximum(i * hb - 1, 0), 0)),
            pl.BlockSpec((1, HALO, d), lambda bi, i: (bi, jnp.minimum((i + 1) * hb, s // HALO - 1), 0)),
            _resident(w_in.shape), _resident(conv_w.shape), _resident(w_ab.shape),
            _resident(a_log.shape), _resident(dt_bias.shape),
            _resident(a_log_t.shape), _resident(dt_bias_t.shape),
        ],
        out_specs=[head(GDN_QK_HEADS), head(GDN_QK_HEADS), head(hv), tok(n_v), gate, gate,
                   pl.BlockSpec((2, 1, tm // CHUNK, hv, CHUNK), lambda bi, i: (0, bi, i, 0, 0))],
        out_shape=[
            jax.ShapeDtypeStruct((bsz, GDN_QK_HEADS, s, GDN_DK), F32),
            jax.ShapeDtypeStruct((bsz, GDN_QK_HEADS, s, GDN_DK), F32),
            jax.ShapeDtypeStruct((bsz, hv, s, GDN_DK), F32),
            jax.ShapeDtypeStruct((bsz, s, n_v), BF16),
            jax.ShapeDtypeStruct((2, bsz, s, hv), F32),
            jax.ShapeDtypeStruct((2, bsz, s, hv), F32),
            jax.ShapeDtypeStruct((2, bsz, s // CHUNK, hv, CHUNK), F32),
        ],
        scratch_shapes=[pltpu.VMEM(((2 * n_qk + n_v) // GDN_DK, tm + 2 * HALO, GDN_DK), F32)],
        compiler_params=_cparams("parallel", "parallel"),
        name="gdn_proj",
    )(x, x, x, w_in, conv_w, w_ab, a_log, dt_bias, a_log_t, dt_bias_t)


def _unit_tri_inverse(a):
    row = lax.broadcasted_iota(jnp.int32, (CHUNK, CHUNK), 0)
    col = lax.broadcasted_iota(jnp.int32, (CHUNK, CHUNK), 1)
    same = lambda s: (row // s) == (col // s)
    eye = (row == col).astype(F32)
    ab = a.astype(BF16)
    zero = jnp.zeros_like(ab)
    adb = jnp.where(same(8), ab, zero)
    ad = adb.astype(F32)
    a2 = _bmm(adb, adb)
    a2b = a2.astype(BF16)
    a4 = _bmm(a2b, a2b)
    p1 = eye - ad + a2 - _bmm(adb, a2b)
    t = p1 + _bmm(p1.astype(BF16), a4.astype(BF16))
    s = 8
    while s < CHUNK:
        off = jnp.logical_and(same(2 * s), jnp.logical_not(same(s)))
        tb = t.astype(BF16)
        x = _bmm(jnp.where(off, ab, zero), tb)
        t = t - _bmm(tb, x.astype(BF16))
        s *= 2
    return t


def _gdn_scan_kernel(q_ref, k_ref, v_ref, gc_ref, beta_ref, gct_ref, o_ref,
                     st_ref, u_ref, wq_ref, attn_ref, ke_ref, cdec_ref, *, reverse):
    @pl.when(pl.program_id(1) == 0)
    def _():
        st_ref[...] = jnp.zeros(st_ref.shape, F32)

    qk_heads, cb, dk = q_ref.shape[1:]
    hv = v_ref.shape[1]
    rep = hv // qk_heads
    nc = cb // CHUNK
    row = lax.broadcasted_iota(jnp.int32, (CHUNK, CHUNK), 0)
    col = lax.broadcasted_iota(jnp.int32, (CHUNK, CHUNK), 1)
    incl = (col >= row) if reverse else (col <= row)
    strict = (col > row) if reverse else (col < row)
    last = 0 if reverse else CHUNK - 1
    lane = lax.broadcasted_iota(jnp.int32, gc_ref.shape[2:], 1)

    def prep(it, carry):
        gc_all = gc_ref[0, 0]
        beta_all = beta_ref[0, 0]
        heads, a_list, rhs_list, late = [], [], [], []
        for pp in range(PREP_QK_HEADS):
            p = it * PREP_QK_HEADS + pp
            q = q_ref[0, p].reshape(nc, CHUNK, dk)
            k = k_ref[0, p].reshape(nc, CHUNK, dk)
            kb = k.astype(BF16)
            kk = _bmm_nt(kb, kb)
            qk = _bmm_nt(q.astype(BF16), kb)
            for hh in range(rep):
                j = p * rep + hh
                pick = lane == j
                gcol = jnp.sum(jnp.where(pick, gc_all, 0.0), axis=1, keepdims=True).reshape(nc, CHUNK, 1)
                bcol = jnp.sum(jnp.where(pick, beta_all, 0.0), axis=1, keepdims=True).reshape(nc, CHUNK, 1)
                grow = gct_ref[0, 0, :, pl.ds(j, 1), :]
                decay = jnp.where(incl, jnp.exp(jnp.where(incl, gcol - grow, 0.0)), 0.0)
                eg = jnp.exp(gcol)
                glast = gcol[:, last:last + 1, :]
                v = v_ref[0, j].reshape(nc, CHUNK, dk)
                heads.append(j)
                a_list.append(jnp.where(strict, bcol * kk * decay, 0.0))
                rhs_list.append(jnp.concatenate([v * bcol, k * (bcol * eg)], axis=-1).astype(BF16))
                late.append(((q * eg).astype(BF16), (qk * decay).astype(BF16),
                             (k * jnp.exp(glast - gcol)).astype(BF16),
                             jnp.broadcast_to(jnp.exp(glast), (nc, 1, dk))))
        t = _unit_tri_inverse(jnp.concatenate(a_list, axis=0)).astype(BF16)
        uw = _bmm(t, jnp.concatenate(rhs_list, axis=0))
        for n, (j, (q_start, attn, k_end, cdec)) in enumerate(zip(heads, late)):
            uw_j = uw[n * nc:(n + 1) * nc]
            u_ref[:, j] = uw_j[:, :, :dk]
            wq_ref[:, j] = jnp.concatenate([uw_j[:, :, dk:].astype(BF16), q_start], axis=1)
            attn_ref[:, j] = attn
            ke_ref[:, j] = k_end
            cdec_ref[:, j] = cdec
        return carry

    lax.fori_loop(0, qk_heads // PREP_QK_HEADS, prep, 0)

    def step(ii, carry):
        i = nc - 1 - ii if reverse else ii
        r0 = pl.multiple_of(i * CHUNK, CHUNK)
        st = st_ref[...]
        ws = _bmm(wq_ref[i], st.astype(BF16))
        v_new = (u_ref[i] - ws[:, :CHUNK]).astype(BF16)
        o_ref[0, :, pl.ds(r0, CHUNK), :] = (ws[:, CHUNK:] + _bmm(attn_ref[i], v_new)).astype(o_ref.dtype)
        st_ref[...] = st * cdec_ref[i] + _bmm_tn(ke_ref[i], v_new)
        return carry

    lax.fori_loop(0, nc, step, 0)


def _gdn_scan(q, k, v, gc, beta, gct, *, direction, cb):
    bsz, qk_heads, s, dk = q.shape
    hv = v.shape[1]
    cb = min(cb, s)
    nb = s // cb
    nc = cb // CHUNK
    reverse = direction == 1
    blk = (lambda n: nb - 1 - n) if reverse else (lambda n: n)
    head_major = lambda h: pl.BlockSpec((1, h, cb, dk), lambda bi, n: (bi, 0, blk(n), 0))
    gate = pl.BlockSpec((1, 1, cb, hv), lambda bi, n: (direction, bi, blk(n), 0))
    return pl.pallas_call(
        functools.partial(_gdn_scan_kernel, reverse=reverse),
        grid=(bsz, nb),
        in_specs=[
            head_major(qk_heads), head_major(qk_heads), head_major(hv), gate, gate,
            pl.BlockSpec((1, 1, nc, hv, CHUNK), lambda bi, n: (direction, bi, blk(n), 0, 0)),
        ],
        out_specs=head_major(hv),
        out_shape=jax.ShapeDtypeStruct((bsz, hv, s, dk), BF16),
        scratch_shapes=[
            pltpu.VMEM((hv, dk, dk), F32),
            pltpu.VMEM((nc, hv, CHUNK, dk), F32),
            pltpu.VMEM((nc, hv, 2 * CHUNK, dk), BF16),
            pltpu.VMEM((nc, hv, CHUNK, CHUNK), BF16),
            pltpu.VMEM((nc, hv, CHUNK, dk), BF16),
            pltpu.VMEM((nc, hv, 1, dk), F32),
        ],
        compiler_params=_cparams("parallel", "arbitrary"),
        name="gdn_scan_rev" if reverse else "gdn_scan_fwd",
    )(q, k, v, gc, beta, gct)


def _prepare_weights(ffn_w_in, ffn_w_out, ln_g, ln_b,
                     gla_w_in, gla_w_gate_down, gla_w_gate_up, gla_b_gate, gla_norm_w, gla_w_out,
                     gdn_w_in, gdn_conv_w, gdn_w_ab, gdn_a_log, gdn_dt_bias, gdn_norm_w, gdn_w_out,
                     xa_w_q, xa_w_kv, xa_w_o):
    n_gla, _, d, rank = gla_w_gate_down.shape
    hk = gla_w_gate_up.shape[-1]
    w_gd = jnp.transpose(gla_w_gate_down, (0, 2, 1, 3)).reshape(n_gla, d, 2 * rank)
    zeros = jnp.zeros((n_gla, rank, hk), F32)
    w_gu = jnp.concatenate([
        jnp.concatenate([gla_w_gate_up[:, 0], zeros], axis=-1),
        jnp.concatenate([zeros, gla_w_gate_up[:, 1]], axis=-1)], axis=1)
    n_gdn = gdn_w_ab.shape[0]
    w_ab = jnp.transpose(gdn_w_ab, (0, 2, 1, 3)).reshape(n_gdn, d, -1)
    return dict(
        ffn_w_in=ffn_w_in.astype(BF16), ffn_w_out=ffn_w_out.astype(BF16),
        ln_g=ln_g[:, :, None, :], ln_b=ln_b[:, :, None, :],
        gla_w_in=gla_w_in.astype(BF16), gla_w_gd=w_gd.astype(BF16), gla_w_gu=w_gu.astype(BF16),
        gla_b_g=gla_b_gate.reshape(n_gla, 1, 2 * hk), gla_norm_w=gla_norm_w[:, None, :],
        gla_w_out=gla_w_out.astype(BF16),
        gdn_w_in=gdn_w_in.astype(BF16), gdn_conv_w=gdn_conv_w,
        gdn_w_ab=jnp.pad(w_ab, ((0, 0), (0, 0), (0, LANES - w_ab.shape[-1]))).astype(BF16),
        gdn_a_log=gdn_a_log, gdn_dt_bias=gdn_dt_bias,
        gdn_a_log_t=gdn_a_log[..., None], gdn_dt_bias_t=gdn_dt_bias[..., None],
        gdn_norm_w=gdn_norm_w[:, None, :], gdn_w_out=gdn_w_out.astype(BF16),
        xa_w_q=xa_w_q.astype(BF16), xa_w_kv=xa_w_kv.astype(BF16), xa_w_o=xa_w_o.astype(BF16),
    )


def _trunk(x, mem, w):
    bsz, s, d = x.shape
    t = bsz * s
    tm = 512
    tm_wide = 2 * SUB_ROWS
    cb = 512
    for i in range(DEPTH):
        ln = lambda n: (w['ln_g'][i, n], w['ln_b'][i, n])
        x2 = _ffn_ln(x.reshape(t, d), w['ffn_w_in'][i, 0], w['ffn_w_out'][i, 0], *ln(0), tm=tm_wide)
        j = i // N_MIXERS
        if i % N_MIXERS == 0:
            q, k, v, r, g = _gla_proj(x2, w['gla_w_in'][j], w['gla_w_gd'][j], w['gla_w_gu'][j],
                                      w['gla_b_g'][j], tm=tm_wide)
            q, k, v, r = (a.reshape(bsz, s, -1) for a in (q, k, v, r))
            g = g.reshape(2, bsz, s, -1)
            o_f = _gla_scan(q, k, v, g, direction=0, cb=cb)
            o_r = _gla_scan(q, k, v, g, direction=1, cb=cb)
            norm_w, w_out = w['gla_norm_w'][j], w['gla_w_out'][j]
        else:
            q, k, v, r, gc, beta, gct = _gdn_proj(
                x2.reshape(bsz, s, d), w['gdn_w_in'][j], w['gdn_conv_w'][j], w['gdn_w_ab'][j],
                w['gdn_a_log'][j], w['gdn_dt_bias'][j],
                w['gdn_a_log_t'][j], w['gdn_dt_bias_t'][j], tm=tm)
            o_f = _gdn_scan(q, k, v, gc, beta, gct, direction=0, cb=cb)
            o_r = _gdn_scan(q, k, v, gc, beta, gct, direction=1, cb=cb)
            norm_w, w_out = w['gdn_norm_w'][j], w['gdn_w_out'][j]
        x2 = _mixer_out(o_f, o_r, r, x2.reshape(bsz, s, d), norm_w, w_out, *ln(1), tm=tm_wide)
        kv = _proj(mem.reshape(-1, d), w['xa_w_kv'][i], tm=256, out_dtype=BF16)
        x3 = _xattn_ln(x2, kv.reshape(bsz, -1, 2 * d),
                       w['xa_w_q'][i], w['xa_w_o'][i], *ln(2), tm=tm_wide)
        x = _ffn_ln(x3.reshape(t, d), w['ffn_w_in'][i, 1], w['ffn_w_out'][i, 1], *ln(3), tm=tm_wide)
        x = x.reshape(bsz, s, d)
    return x


def kernel(x_prompt, x_sample, mem_prompt, mem_sample, ffn_w_in, ffn_w_out, ln_g, ln_b, gla_w_in, gla_w_gate_down, gla_w_gate_up, gla_b_gate, gla_norm_w, gla_w_out, gdn_w_in, gdn_conv_w, gdn_w_ab, gdn_a_log, gdn_dt_bias, gdn_norm_w, gdn_w_out, xa_w_q, xa_w_kv, xa_w_o):
    w = _prepare_weights(ffn_w_in, ffn_w_out, ln_g, ln_b,
                         gla_w_in, gla_w_gate_down, gla_w_gate_up, gla_b_gate, gla_norm_w, gla_w_out,
                         gdn_w_in, gdn_conv_w, gdn_w_ab, gdn_a_log, gdn_dt_bias, gdn_norm_w, gdn_w_out,
                         xa_w_q, xa_w_kv, xa_w_o)
    return (_trunk(x_prompt, mem_prompt, w), _trunk(x_sample, mem_sample, w))
```

```python
import functools

import jax
import jax.numpy as jnp
from jax import lax
from jax.experimental import pallas as pl
from jax.experimental.pallas import tpu as pltpu

F32 = jnp.float32
BF16 = jnp.bfloat16

DEPTH = 2
N_MIXERS = 2
CHUNK = 64
GLA_HEADS = 4
GLA_GATE_NORMALIZER = 16.0
GDN_QK_HEADS = 8
GDN_V_HEADS = 16
GDN_DK = 128
GDN_CONV = 4
XA_HEADS = 4
ALPHA = (2.0 * DEPTH) ** 0.25
LN_EPS = 1e-5
NORM_EPS = 1e-6

VMEM_LIMIT_BYTES = 56 * 1024 * 1024
LANES = 128
MXU_DIM = 256
HALO = 8
SUB_ROWS = 512
SCAN_GROUP_CHUNKS = 2


def _cparams(*sem):
    return pltpu.CompilerParams(dimension_semantics=sem, vmem_limit_bytes=VMEM_LIMIT_BYTES)


def _resident(shape):
    nd = len(shape)
    return pl.BlockSpec(shape, lambda *_: (0,) * nd, pipeline_mode=pl.Buffered(1))


def _dot(a, b):
    return jnp.dot(a, b, preferred_element_type=F32)


def _dot_nt(a, b):
    return lax.dot_general(a, b, (((1,), (1,)), ((), ())), preferred_element_type=F32)


def _dot_tn(a, b):
    return lax.dot_general(a, b, (((0,), (0,)), ((), ())), preferred_element_type=F32)


def _bmm(a, b):
    return jnp.einsum('nmk,nkp->nmp', a, b, preferred_element_type=F32)


def _bmm_nt(a, b):
    return jnp.einsum('nmk,npk->nmp', a, b, preferred_element_type=F32)


def _bmm_tn(a, b):
    return jnp.einsum('nkm,nkp->nmp', a, b, preferred_element_type=F32)


def _sigmoid(x):
    return jax.nn.sigmoid(x)


def _silu(x):
    return x * _sigmoid(x)


def _softplus(x):
    return jnp.maximum(x, 0.0) + jnp.log1p(jnp.exp(-jnp.abs(x)))


def _layer_norm(y, g, b):
    mu = jnp.mean(y, axis=-1, keepdims=True)
    yc = y - mu
    var = jnp.mean(yc * yc, axis=-1, keepdims=True)
    return yc * lax.rsqrt(var + LN_EPS) * g + b


def _chunk_cumsum(x, axis, reverse):
    n = x.shape[axis]
    pos = lax.broadcasted_iota(jnp.int32, x.shape, axis) % CHUNK
    s = 1
    while s < CHUNK:
        if reverse:
            shifted = pltpu.roll(x, n - s, axis)
            keep = pos < CHUNK - s
        else:
            shifted = pltpu.roll(x, s, axis)
            keep = pos >= s
        x = x + jnp.where(keep, shifted, 0.0)
        s *= 2
    return x


def _ffn_ln_kernel(x_ref, win_ref, wout_ref, g_ref, b_ref, o_ref, *, d_ff, n_split, sub):
    tiles = d_ff // MXU_DIM
    cuts = [MXU_DIM * (tiles * c // n_split) for c in range(n_split)] + [d_ff]
    for r0 in range(0, x_ref.shape[0], sub):
        x = x_ref[r0:r0 + sub, :]
        xb = x.astype(BF16)
        acc = jnp.zeros(x.shape, F32)
        for lo, hi in zip(cuts[:-1], cuts[1:]):
            gate = _dot(xb, win_ref[:, lo:hi])
            up = _dot(xb, win_ref[:, d_ff + lo:d_ff + hi])
            h = (_silu(gate) * up).astype(BF16)
            acc = acc + _dot(h, wout_ref[lo:hi, :])
        y = ALPHA * x + 0.5 * acc
        o_ref[r0:r0 + sub, :] = _layer_norm(y, g_ref[...], b_ref[...])


def _ffn_ln(x, w_in, w_out, g, b, *, tm):
    t, d = x.shape
    d_ff = w_out.shape[0]
    tm = min(tm, t)
    return pl.pallas_call(
        functools.partial(_ffn_ln_kernel, d_ff=d_ff, n_split=2, sub=min(SUB_ROWS, tm)),
        grid=(t // tm,),
        in_specs=[
            pl.BlockSpec((tm, d), lambda i: (i, 0)),
            _resident(w_in.shape),
            _resident(w_out.shape),
            _resident(g.shape),
            _resident(b.shape),
        ],
        out_specs=pl.BlockSpec((tm, d), lambda i: (i, 0)),
        out_shape=jax.ShapeDtypeStruct((t, d), F32),
        compiler_params=_cparams("parallel"),
        name="ffn_ln",
    )(x, w_in, w_out, g, b)


def _proj_kernel(x_ref, w_ref, o_ref):
    o_ref[...] = _dot(x_ref[...].astype(BF16), w_ref[...]).astype(o_ref.dtype)


def _proj(x, w, *, tm, out_dtype):
    t, d = x.shape
    n = w.shape[1]
    tm = min(tm, t)
    return pl.pallas_call(
        _proj_kernel,
        grid=(t // tm,),
        in_specs=[pl.BlockSpec((tm, d), lambda i: (i, 0)), _resident(w.shape)],
        out_specs=pl.BlockSpec((tm, n), lambda i: (i, 0)),
        out_shape=jax.ShapeDtypeStruct((t, n), out_dtype),
        compiler_params=_cparams("parallel"),
        name="mem_kv_proj",
    )(x, w)


def _xattn_ln_kernel(x_ref, kv_ref, wq_ref, wo_ref, g_ref, b_ref, o_ref, *, heads, sub):
    d = x_ref.shape[-1]
    dh = d // heads
    for r0 in range(0, x_ref.shape[1], sub):
        x = x_ref[0, r0:r0 + sub, :]
        q = _dot(x.astype(BF16), wq_ref[...]).astype(BF16)
        outs = []
        for h in range(heads):
            qh = q[:, h * dh:(h + 1) * dh]
            kh = kv_ref[0, :, h * dh:(h + 1) * dh]
            vh = kv_ref[0, :, d + h * dh:d + (h + 1) * dh]
            s = _dot_nt(qh, kh) * (dh ** -0.5)
            e = jnp.exp(s - jnp.max(s, axis=-1, keepdims=True))
            den = jnp.sum(e, axis=-1, keepdims=True)
            outs.append(_dot(e.astype(BF16), vh) / den)
        o = jnp.concatenate(outs, axis=-1).astype(BF16)
        y = ALPHA * x + _dot(o, wo_ref[...])
        o_ref[0, r0:r0 + sub, :] = _layer_norm(y, g_ref[...], b_ref[...])


def _xattn_ln(x, kv, w_q, w_o, g, b, *, tm):
    bsz, s, d = x.shape
    n_mem = kv.shape[1]
    tm = min(tm, s)
    return pl.pallas_call(
        functools.partial(_xattn_ln_kernel, heads=XA_HEADS, sub=min(SUB_ROWS, tm)),
        grid=(bsz, s // tm),
        in_specs=[
            pl.BlockSpec((1, tm, d), lambda bi, i: (bi, i, 0)),
            pl.BlockSpec((1, n_mem, 2 * d), lambda bi, i: (bi, 0, 0)),
            _resident(w_q.shape),
            _resident(w_o.shape),
            _resident(g.shape),
            _resident(b.shape),
        ],
        out_specs=pl.BlockSpec((1, tm, d), lambda bi, i: (bi, i, 0)),
        out_shape=jax.ShapeDtypeStruct((bsz, s, d), F32),
        compiler_params=_cparams("parallel", "parallel"),
        name="xattn_ln",
    )(x, kv, w_q, w_o, g, b)


def _mixer_out_kernel(of_ref, or_ref, r_ref, x_ref, nw_ref, wo_ref, g_ref, b_ref, o_ref, *, dv, head_major, sub):
    nw = nw_ref[...]
    for r0 in range(0, x_ref.shape[1], sub):
        rows = slice(r0, r0 + sub)
        parts = []
        for h in range(r_ref.shape[-1] // dv):
            if head_major:
                oh = of_ref[0, h, rows, :].astype(F32) + or_ref[0, h, rows, :].astype(F32)
            else:
                oh = (of_ref[0, rows, h * dv:(h + 1) * dv].astype(F32)
                      + or_ref[0, rows, h * dv:(h + 1) * dv].astype(F32))
            ms = jnp.mean(oh * oh, axis=-1, keepdims=True)
            parts.append(oh * lax.rsqrt(ms + NORM_EPS) * nw)
        gated = (jnp.concatenate(parts, axis=-1) * _silu(r_ref[0, rows, :].astype(F32))).astype(BF16)
        y = ALPHA * x_ref[0, rows, :] + _dot(gated, wo_ref[...])
        o_ref[0, rows, :] = _layer_norm(y, g_ref[...], b_ref[...])


def _mixer_out(o_f, o_r, r, x, norm_w, w_out, g, b, *, tm):
    bsz, s, d = x.shape
    dh = r.shape[-1]
    dv = norm_w.shape[-1]
    tm = min(tm, s)
    head_major = o_f.ndim == 4
    tok = lambda n: pl.BlockSpec((1, tm, n), lambda bi, i: (bi, i, 0))
    o_spec = pl.BlockSpec((1, dh // dv, tm, dv), lambda bi, i: (bi, 0, i, 0)) if head_major else tok(dh)
    return pl.pallas_call(
        functools.partial(_mixer_out_kernel, dv=dv, head_major=head_major, sub=min(SUB_ROWS, tm)),
        grid=(bsz, s // tm),
        in_specs=[
            o_spec, o_spec, tok(dh), tok(d),
            _resident(norm_w.shape),
            _resident(w_out.shape),
            _resident(g.shape),
            _resident(b.shape),
        ],
        out_specs=tok(d),
        out_shape=jax.ShapeDtypeStruct((bsz, s, d), F32),
        compiler_params=_cparams("parallel", "parallel"),
        name="mixer_out_ln",
    )(o_f, o_r, r, x, norm_w, w_out, g, b)


def _gla_proj_kernel(x_ref, win_ref, wgd_ref, wgu_ref, bg_ref,
                     q_ref, k_ref, v_ref, r_ref, g_ref, *, hk, hv, sub):
    dk = hk // GLA_HEADS
    for r0 in range(0, x_ref.shape[0], sub):
        rows = slice(r0, r0 + sub)
        xb = x_ref[rows, :].astype(BF16)
        q_ref[rows, :] = _dot(xb, win_ref[:, 0:hk]) * (dk ** -0.5)
        k_ref[rows, :] = _dot(xb, win_ref[:, hk:2 * hk])
        v_ref[rows, :] = _dot(xb, win_ref[:, 2 * hk:2 * hk + hv])
        r_ref[rows, :] = _dot(xb, win_ref[:, 2 * hk + hv:2 * hk + 2 * hv]).astype(r_ref.dtype)
        low = _dot(xb, wgd_ref[...]).astype(BF16)
        logit = _dot(low, wgu_ref[...]) + bg_ref[...]
        logsig = jnp.minimum(logit, 0.0) - jnp.log1p(jnp.exp(-jnp.abs(logit)))
        gate = logsig / GLA_GATE_NORMALIZER
        g_ref[0, rows, :] = gate[:, 0:hk]
        g_ref[1, rows, :] = gate[:, hk:2 * hk]


def _gla_proj(x, w_in, w_gd, w_gu, b_g, *, tm):
    t, d = x.shape
    hk = w_gu.shape[1] // 2
    hv = (w_in.shape[1] - 2 * hk) // 2
    tm = min(tm, t)
    row = lambda n: pl.BlockSpec((tm, n), lambda i: (i, 0))
    return pl.pallas_call(
        functools.partial(_gla_proj_kernel, hk=hk, hv=hv, sub=min(SUB_ROWS, tm)),
        grid=(t // tm,),
        in_specs=[row(d), _resident(w_in.shape), _resident(w_gd.shape),
                  _resident(w_gu.shape), _resident(b_g.shape)],
        out_specs=[row(hk), row(hk), row(hv), row(hv),
                   pl.BlockSpec((2, tm, hk), lambda i: (0, i, 0))],
        out_shape=[
            jax.ShapeDtypeStruct((t, hk), F32),
            jax.ShapeDtypeStruct((t, hk), F32),
            jax.ShapeDtypeStruct((t, hv), F32),
            jax.ShapeDtypeStruct((t, hv), BF16),
            jax.ShapeDtypeStruct((2, t, hk), F32),
        ],
        compiler_params=_cparams("parallel"),
        name="gla_proj",
    )(x, w_in, w_gd, w_gu, b_g)


def _gla_scan_kernel(q_ref, k_ref, v_ref, g_ref, o_ref, st_ref, *, reverse):
    @pl.when(pl.program_id(1) == 0)
    def _():
        st_ref[...] = jnp.zeros(st_ref.shape, F32)

    heads, dv, dk = st_ref.shape
    cb = q_ref.shape[1]
    nc = cb // CHUNK

    def by_head(a, width):
        return jnp.concatenate([a[:, h * width:(h + 1) * width].reshape(nc, CHUNK, width)
                                for h in range(heads)], axis=0)

    b = by_head(_chunk_cumsum(g_ref[0, 0], 0, reverse), dk)
    q = by_head(q_ref[0], dk)
    k = by_head(k_ref[0], dk)
    vb = by_head(v_ref[0].astype(BF16), dv)
    mid = CHUNK // 2 if reverse else CHUNK // 2 - 1
    last = 0 if reverse else CHUNK - 1
    b_mid = b[:, mid:mid + 1, :]
    b_last = b[:, last:last + 1, :]

    qe = (q * jnp.exp(b - b_mid)).astype(BF16)
    ke = (k * jnp.exp(b_mid - b)).astype(BF16)
    scores = _bmm_nt(qe, ke)
    row = lax.broadcasted_iota(jnp.int32, (CHUNK, CHUNK), 0)
    col = lax.broadcasted_iota(jnp.int32, (CHUNK, CHUNK), 1)
    visible = (col >= row) if reverse else (col <= row)
    o = _bmm(jnp.where(visible, scores, 0.0).astype(BF16), vb)

    q_start = (q * jnp.exp(b)).astype(BF16)
    k_end = (k * jnp.exp(b_last - b)).astype(BF16)
    decay = jnp.exp(b_last)
    zt = _bmm_tn(vb, k_end)
    states = [None] * (heads * nc)
    finals = []
    for h in range(heads):
        st = st_ref[h]
        for i in (range(nc - 1, -1, -1) if reverse else range(nc)):
            states[h * nc + i] = st.astype(BF16)
            st = st * decay[h * nc + i] + zt[h * nc + i]
        finals.append(st)
    o = o + _bmm_nt(q_start, jnp.stack(states))
    for h in range(heads):
        st_ref[h] = finals[h]
        o_ref[0, :, h * dv:(h + 1) * dv] = o[h * nc:(h + 1) * nc].reshape(cb, dv).astype(o_ref.dtype)


def _gla_scan(q, k, v, g, *, direction, cb):
    bsz, s, hk = q.shape
    hv = v.shape[-1]
    dk, dv = hk // GLA_HEADS, hv // GLA_HEADS
    cb = min(cb, s)
    nb = s // cb
    reverse = direction == 1
    blk = (lambda n: nb - 1 - n) if reverse else (lambda n: n)
    tok = lambda n: pl.BlockSpec((1, cb, n), lambda bi, n_: (bi, blk(n_), 0))
    return pl.pallas_call(
        functools.partial(_gla_scan_kernel, reverse=reverse),
        grid=(bsz, nb),
        in_specs=[tok(hk), tok(hk), tok(hv),
                  pl.BlockSpec((1, 1, cb, hk), lambda bi, n: (direction, bi, blk(n), 0))],
        out_specs=tok(hv),
        out_shape=jax.ShapeDtypeStruct((bsz, s, hv), BF16),
        scratch_shapes=[pltpu.VMEM((GLA_HEADS, dv, dk), F32)],
        compiler_params=_cparams("parallel", "arbitrary"),
        name="gla_scan_rev" if reverse else "gla_scan_fwd",
    )(q, k, v, g)


def _gdn_proj_kernel(x_ref, xp_ref, xn_ref, win_ref, cw_ref, wab_ref,
                     alog_ref, dtb_ref, alogt_ref, dtbt_ref,
                     q_ref, k_ref, v_ref, z_ref, gc_ref, beta_ref, gct_ref, p_ref, *, n_qk, n_v):
    i = pl.program_id(1)
    tm = x_ref.shape[1]
    xb = x_ref[0].astype(BF16)
    keep_prev = (i > 0).astype(F32)
    keep_next = (i < pl.num_programs(1) - 1).astype(F32)
    xx = jnp.concatenate([(xp_ref[0] * keep_prev).astype(BF16), xb,
                          (xn_ref[0] * keep_next).astype(BF16)], axis=0)
    n_conv = 2 * n_qk + n_v
    z_ref[0] = _dot(xb, win_ref[:, n_conv:n_conv + n_v]).astype(z_ref.dtype)

    hv = GDN_V_HEADS
    ab = _dot(xb, wab_ref[...])
    abt = ab.T
    for d in range(2):
        a = ab[:, 2 * d * hv:(2 * d + 1) * hv]
        bt = ab[:, (2 * d + 1) * hv:(2 * d + 2) * hv]
        g = -jnp.exp(alog_ref[d:d + 1, :]) * _softplus(a + dtb_ref[d:d + 1, :])
        gc_ref[d, 0] = _chunk_cumsum(g, 0, d == 1)
        beta_ref[d, 0] = _sigmoid(bt)
        at = abt[2 * d * hv:(2 * d + 1) * hv, :]
        gt = -jnp.exp(alogt_ref[d]) * _softplus(at + dtbt_ref[d])
        gct = _chunk_cumsum(gt, 1, d == 1)
        for c in range(tm // CHUNK):
            gct_ref[d, 0, c] = gct[:, c * CHUNK:(c + 1) * CHUNK]

    for j in range(n_conv // GDN_DK):
        cs = slice(j * GDN_DK, (j + 1) * GDN_DK)
        if j % 2 == 0:
            pg = _dot(xx, win_ref[:, j * GDN_DK:(j + 2) * GDN_DK])
            p_ref[j] = pg[:, :GDN_DK]
            p_ref[j + 1] = pg[:, GDN_DK:]
        y = jnp.zeros((tm, GDN_DK), F32)
        for tap in range(GDN_CONV):
            y = y + cw_ref[tap:tap + 1, cs] * p_ref[j, pl.ds(HALO - 2 + tap, tm), :]
        y = _silu(y)
        if j < 2 * GDN_QK_HEADS:
            y = y * lax.rsqrt(jnp.sum(y * y, axis=-1, keepdims=True) + NORM_EPS)
            if j < GDN_QK_HEADS:
                q_ref[0, j] = y * (GDN_DK ** -0.5)
            else:
                k_ref[0, j - GDN_QK_HEADS] = y
        else:
            v_ref[0, j - 2 * GDN_QK_HEADS] = y


def _gdn_proj(x, w_in, conv_w, w_ab, a_log, dt_bias, a_log_t, dt_bias_t, *, tm):
    bsz, s, d = x.shape
    hv = GDN_V_HEADS
    n_qk = GDN_QK_HEADS * GDN_DK
    n_v = hv * GDN_DK
    tm = min(tm, s)
    nt = s // tm
    hb = tm // HALO
    tok = lambda n: pl.BlockSpec((1, tm, n), lambda bi, i: (bi, i, 0))
    head = lambda n: pl.BlockSpec((1, n, tm, GDN_DK), lambda bi, i: (bi, 0, i, 0))
    gate = pl.BlockSpec((2, 1, tm, hv), lambda bi, i: (0, bi, i, 0))
    return pl.pallas_call(
        functools.partial(_gdn_proj_kernel, n_qk=n_qk, n_v=n_v),
        grid=(bsz, nt),
        in_specs=[
            tok(d),
            pl.BlockSpec((1, HALO, d), lambda bi, i: (bi, jnp.maximum(i * hb - 1, 0), 0)),
            pl.BlockSpec((1, HALO, d), lambda bi, i: (bi, jnp.minimum((i + 1) * hb, s // HALO - 1), 0)),
            _resident(w_in.shape), _resident(conv_w.shape), _resident(w_ab.shape),
            _resident(a_log.shape), _resident(dt_bias.shape),
            _resident(a_log_t.shape), _resident(dt_bias_t.shape),
        ],
        out_specs=[head(GDN_QK_HEADS), head(GDN_QK_HEADS), head(hv), tok(n_v), gate, gate,
                   pl.BlockSpec((2, 1, tm // CHUNK, hv, CHUNK), lambda bi, i: (0, bi, i, 0, 0))],
        out_shape=[
            jax.ShapeDtypeStruct((bsz, GDN_QK_HEADS, s, GDN_DK), F32),
            jax.ShapeDtypeStruct((bsz, GDN_QK_HEADS, s, GDN_DK), F32),
            jax.ShapeDtypeStruct((bsz, hv, s, GDN_DK), F32),
            jax.ShapeDtypeStruct((bsz, s, n_v), BF16),
            jax.ShapeDtypeStruct((2, bsz, s, hv), F32),
            jax.ShapeDtypeStruct((2, bsz, s, hv), F32),
            jax.ShapeDtypeStruct((2, bsz, s // CHUNK, hv, CHUNK), F32),
        ],
        scratch_shapes=[pltpu.VMEM(((2 * n_qk + n_v) // GDN_DK, tm + 2 * HALO, GDN_DK), F32)],
        compiler_params=_cparams("parallel", "parallel"),
        name="gdn_proj",
    )(x, x, x, w_in, conv_w, w_ab, a_log, dt_bias, a_log_t, dt_bias_t)


def _unit_tri_inverse(a):
    row = lax.broadcasted_iota(jnp.int32, (CHUNK, CHUNK), 0)
    col = lax.broadcasted_iota(jnp.int32, (CHUNK, CHUNK), 1)
    same = lambda s: (row // s) == (col // s)
    eye = (row == col).astype(F32)
    ab = a.astype(BF16)
    zero = jnp.zeros_like(ab)
    adb = jnp.where(same(8), ab, zero)
    ad = adb.astype(F32)
    a2 = _bmm(adb, adb)
    a2b = a2.astype(BF16)
    a4 = _bmm(a2b, a2b)
    p1 = eye - ad + a2 - _bmm(adb, a2b)
    t = p1 + _bmm(p1.astype(BF16), a4.astype(BF16))
    s = 8
    while s < CHUNK:
        off = jnp.logical_and(same(2 * s), jnp.logical_not(same(s)))
        tb = t.astype(BF16)
        x = _bmm(jnp.where(off, ab, zero), tb)
        t = t - _bmm(tb, x.astype(BF16))
        s *= 2
    return t


def _gdn_scan_kernel(q_ref, k_ref, v_ref, gc_ref, beta_ref, gct_ref, o_ref,
                     st_ref, u_ref, wq_ref, attn_ref, ke_ref, cdec_ref, *, reverse):
    @pl.when(pl.program_id(1) == 0)
    def _():
        st_ref[...] = jnp.zeros(st_ref.shape, F32)

    qk_heads, cb, dk = q_ref.shape[1:]
    hv = v_ref.shape[1]
    rep = hv // qk_heads
    nc = cb // CHUNK
    row = lax.broadcasted_iota(jnp.int32, (CHUNK, CHUNK), 0)
    col = lax.broadcasted_iota(jnp.int32, (CHUNK, CHUNK), 1)
    incl = (col >= row) if reverse else (col <= row)
    strict = (col > row) if reverse else (col < row)
    last = 0 if reverse else CHUNK - 1
    gn = SCAN_GROUP_CHUNKS
    groups = nc // gn

    def prep(g):
        chunks = slice(g * gn, (g + 1) * gn)
        rows = slice(g * gn * CHUNK, (g + 1) * gn * CHUNK)
        gc_all = gc_ref[0, 0, rows, :]
        beta_all = beta_ref[0, 0, rows, :]
        heads, a_list, rhs_list, late = [], [], [], []
        for p in range(qk_heads):
            q = q_ref[0, p, rows, :].reshape(gn, CHUNK, dk)
            k = k_ref[0, p, rows, :].reshape(gn, CHUNK, dk)
            kb = k.astype(BF16)
            kk = _bmm_nt(kb, kb)
            qk = _bmm_nt(q.astype(BF16), kb)
            for hh in range(rep):
                j = p * rep + hh
                gcol = gc_all[:, j:j + 1].reshape(gn, CHUNK, 1)
                bcol = beta_all[:, j:j + 1].reshape(gn, CHUNK, 1)
                grow = gct_ref[0, 0, chunks, j:j + 1, :]
                decay = jnp.where(incl, jnp.exp(jnp.where(incl, gcol - grow, 0.0)), 0.0)
                eg = jnp.exp(gcol)
                glast = gcol[:, last:last + 1, :]
                v = v_ref[0, j, rows, :].reshape(gn, CHUNK, dk)
                heads.append(j)
                a_list.append(jnp.where(strict, bcol * kk * decay, 0.0))
                rhs_list.append(jnp.concatenate([v * bcol, k * (bcol * eg)], axis=-1).astype(BF16))
                late.append(((q * eg).astype(BF16), (qk * decay).astype(BF16),
                             (k * jnp.exp(glast - gcol)).astype(BF16),
                             jnp.broadcast_to(jnp.exp(glast), (gn, 1, dk))))
        t = _unit_tri_inverse(jnp.concatenate(a_list, axis=0)).astype(BF16)
        uw = _bmm(t, jnp.concatenate(rhs_list, axis=0))
        for n, (j, (q_start, attn, k_end, cdec)) in enumerate(zip(heads, late)):
            uw_j = uw[n * gn:(n + 1) * gn]
            u_ref[chunks, j] = uw_j[:, :, :dk]
            wq_ref[chunks, j] = jnp.concatenate([uw_j[:, :, dk:].astype(BF16), q_start], axis=1)
            attn_ref[chunks, j] = attn
            ke_ref[chunks, j] = k_end
            cdec_ref[chunks, j] = cdec

    def scan(g):
        for c in (range(gn - 1, -1, -1) if reverse else range(gn)):
            i = g * gn + c
            st = st_ref[...]
            ws = _bmm(wq_ref[i], st.astype(BF16))
            v_new = (u_ref[i] - ws[:, :CHUNK]).astype(BF16)
            o_ref[0, :, i * CHUNK:(i + 1) * CHUNK, :] = (ws[:, CHUNK:] + _bmm(attn_ref[i], v_new)).astype(o_ref.dtype)
            st_ref[...] = st * cdec_ref[i] + _bmm_tn(ke_ref[i], v_new)

    order = list(range(groups - 1, -1, -1) if reverse else range(groups))
    prep(order[0])
    for prev, cur in zip(order[:-1], order[1:]):
        prep(cur)
        scan(prev)
    scan(order[-1])


def _gdn_scan(q, k, v, gc, beta, gct, *, direction, cb):
    bsz, qk_heads, s, dk = q.shape
    hv = v.shape[1]
    cb = min(cb, s)
    nb = s // cb
    nc = cb // CHUNK
    reverse = direction == 1
    blk = (lambda n: nb - 1 - n) if reverse else (lambda n: n)
    head_major = lambda h: pl.BlockSpec((1, h, cb, dk), lambda bi, n: (bi, 0, blk(n), 0))
    gate = pl.BlockSpec((1, 1, cb, hv), lambda bi, n: (direction, bi, blk(n), 0))
    return pl.pallas_call(
        functools.partial(_gdn_scan_kernel, reverse=reverse),
        grid=(bsz, nb),
        in_specs=[
            head_major(qk_heads), head_major(qk_heads), head_major(hv), gate, gate,
            pl.BlockSpec((1, 1, nc, hv, CHUNK), lambda bi, n: (direction, bi, blk(n), 0, 0)),
        ],
        out_specs=head_major(hv),
        out_shape=jax.ShapeDtypeStruct((bsz, hv, s, dk), BF16),
        scratch_shapes=[
            pltpu.VMEM((hv, dk, dk), F32),
            pltpu.VMEM((nc, hv, CHUNK, dk), F32),
            pltpu.VMEM((nc, hv, 2 * CHUNK, dk), BF16),
            pltpu.VMEM((nc, hv, CHUNK, CHUNK), BF16),
            pltpu.VMEM((nc, hv, CHUNK, dk), BF16),
            pltpu.VMEM((nc, hv, 1, dk), F32),
        ],
        compiler_params=_cparams("parallel", "arbitrary"),
        name="gdn_scan_rev" if reverse else "gdn_scan_fwd",
    )(q, k, v, gc, beta, gct)


def _prepare_weights(ffn_w_in, ffn_w_out, ln_g, ln_b,
                     gla_w_in, gla_w_gate_down, gla_w_gate_up, gla_b_gate, gla_norm_w, gla_w_out,
                     gdn_w_in, gdn_conv_w, gdn_w_ab, gdn_a_log, gdn_dt_bias, gdn_norm_w, gdn_w_out,
                     xa_w_q, xa_w_kv, xa_w_o):
    n_gla, _, d, rank = gla_w_gate_down.shape
    hk = gla_w_gate_up.shape[-1]
    w_gd = jnp.transpose(gla_w_gate_down, (0, 2, 1, 3)).reshape(n_gla, d, 2 * rank)
    zeros = jnp.zeros((n_gla, rank, hk), F32)
    w_gu = jnp.concatenate([
        jnp.concatenate([gla_w_gate_up[:, 0], zeros], axis=-1),
        jnp.concatenate([zeros, gla_w_gate_up[:, 1]], axis=-1)], axis=1)
    n_gdn = gdn_w_ab.shape[0]
    w_ab = jnp.transpose(gdn_w_ab, (0, 2, 1, 3)).reshape(n_gdn, d, -1)
    return dict(
        ffn_w_in=ffn_w_in.astype(BF16), ffn_w_out=ffn_w_out.astype(BF16),
        ln_g=ln_g[:, :, None, :], ln_b=ln_b[:, :, None, :],
        gla_w_in=gla_w_in.astype(BF16), gla_w_gd=w_gd.astype(BF16), gla_w_gu=w_gu.astype(BF16),
        gla_b_g=gla_b_gate.reshape(n_gla, 1, 2 * hk), gla_norm_w=gla_norm_w[:, None, :],
        gla_w_out=gla_w_out.astype(BF16),
        gdn_w_in=gdn_w_in.astype(BF16), gdn_conv_w=gdn_conv_w,
        gdn_w_ab=jnp.pad(w_ab, ((0, 0), (0, 0), (0, LANES - w_ab.shape[-1]))).astype(BF16),
        gdn_a_log=gdn_a_log, gdn_dt_bias=gdn_dt_bias,
        gdn_a_log_t=gdn_a_log[..., None], gdn_dt_bias_t=gdn_dt_bias[..., None],
        gdn_norm_w=gdn_norm_w[:, None, :], gdn_w_out=gdn_w_out.astype(BF16),
        xa_w_q=xa_w_q.astype(BF16), xa_w_kv=xa_w_kv.astype(BF16), xa_w_o=xa_w_o.astype(BF16),
    )


def _trunk(x, mem, w):
    bsz, s, d = x.shape
    t = bsz * s
    tm = 512
    tm_wide = 2 * SUB_ROWS
    cb = 512
    for i in range(DEPTH):
        ln = lambda n: (w['ln_g'][i, n], w['ln_b'][i, n])
        x2 = _ffn_ln(x.reshape(t, d), w['ffn_w_in'][i, 0], w['ffn_w_out'][i, 0], *ln(0), tm=tm_wide)
        j = i // N_MIXERS
        if i % N_MIXERS == 0:
            q, k, v, r, g = _gla_proj(x2, w['gla_w_in'][j], w['gla_w_gd'][j], w['gla_w_gu'][j],
                                      w['gla_b_g'][j], tm=tm_wide)
            q, k, v, r = (a.reshape(bsz, s, -1) for a in (q, k, v, r))
            g = g.reshape(2, bsz, s, -1)
            o_f = _gla_scan(q, k, v, g, direction=0, cb=2 * cb)
            o_r = _gla_scan(q, k, v, g, direction=1, cb=2 * cb)
            norm_w, w_out = w['gla_norm_w'][j], w['gla_w_out'][j]
        else:
            q, k, v, r, gc, beta, gct = _gdn_proj(
                x2.reshape(bsz, s, d), w['gdn_w_in'][j], w['gdn_conv_w'][j], w['gdn_w_ab'][j],
                w['gdn_a_log'][j], w['gdn_dt_bias'][j],
                w['gdn_a_log_t'][j], w['gdn_dt_bias_t'][j], tm=tm)
            o_f = _gdn_scan(q, k, v, gc, beta, gct, direction=0, cb=cb)
            o_r = _gdn_scan(q, k, v, gc, beta, gct, direction=1, cb=cb)
            norm_w, w_out = w['gdn_norm_w'][j], w['gdn_w_out'][j]
        x2 = _mixer_out(o_f, o_r, r, x2.reshape(bsz, s, d), norm_w, w_out, *ln(1), tm=tm_wide)
        kv = _proj(mem.reshape(-1, d), w['xa_w_kv'][i], tm=256, out_dtype=BF16)
        x3 = _xattn_ln(x2, kv.reshape(bsz, -1, 2 * d),
                       w['xa_w_q'][i], w['xa_w_o'][i], *ln(2), tm=tm_wide)
        x = _ffn_ln(x3.reshape(t, d), w['ffn_w_in'][i, 1], w['ffn_w_out'][i, 1], *ln(3), tm=tm_wide)
        x = x.reshape(bsz, s, d)
    return x


def kernel(x_prompt, x_sample, mem_prompt, mem_sample, ffn_w_in, ffn_w_out, ln_g, ln_b, gla_w_in, gla_w_gate_down, gla_w_gate_up, gla_b_gate, gla_norm_w, gla_w_out, gdn_w_in, gdn_conv_w, gdn_w_ab, gdn_a_log, gdn_dt_bias, gdn_norm_w, gdn_w_out, xa_w_q, xa_w_kv, xa_w_o):
    w = _prepare_weights(ffn_w_in, ffn_w_out, ln_g, ln_b,
                         gla_w_in, gla_w_gate_down, gla_w_gate_up, gla_b_gate, gla_norm_w, gla_w_out,
                         gdn_w_in, gdn_conv_w, gdn_w_ab, gdn_a_log, gdn_dt_bias, gdn_norm_w, gdn_w_out,
                         xa_w_q, xa_w_kv, xa_w_o)
    return (_trunk(x_prompt, mem_prompt, w), _trunk(x_sample, mem_sample, w))
```

```python
import functools

import jax
import jax.numpy as jnp
from jax import lax
from jax.experimental import pallas as pl
from jax.experimental.pallas import tpu as pltpu

F32 = jnp.float32
BF16 = jnp.bfloat16

DEPTH = 2
N_MIXERS = 2
CHUNK = 64
GLA_HEADS = 4
GLA_GATE_NORMALIZER = 16.0
GDN_QK_HEADS = 8
GDN_V_HEADS = 16
GDN_DK = 128
GDN_CONV = 4
XA_HEADS = 4
ALPHA = (2.0 * DEPTH) ** 0.25
LN_EPS = 1e-5
NORM_EPS = 1e-6

VMEM_LIMIT_BYTES = 56 * 1024 * 1024
LANES = 128
MXU_DIM = 256
HALO = 8
SUB_ROWS = 512
FFN_SUB_ROWS = 256
SCAN_GROUP_CHUNKS = 2


def _cparams(*sem):
    return pltpu.CompilerParams(dimension_semantics=sem, vmem_limit_bytes=VMEM_LIMIT_BYTES)


def _resident(shape):
    nd = len(shape)
    return pl.BlockSpec(shape, lambda *_: (0,) * nd, pipeline_mode=pl.Buffered(1))


def _dot(a, b):
    return jnp.dot(a, b, preferred_element_type=F32)


def _dot_nt(a, b):
    return lax.dot_general(a, b, (((1,), (1,)), ((), ())), preferred_element_type=F32)


def _dot_tn(a, b):
    return lax.dot_general(a, b, (((0,), (0,)), ((), ())), preferred_element_type=F32)


def _bmm(a, b):
    return jnp.einsum('nmk,nkp->nmp', a, b, preferred_element_type=F32)


def _bmm_nt(a, b):
    return jnp.einsum('nmk,npk->nmp', a, b, preferred_element_type=F32)


def _bmm_tn(a, b):
    return jnp.einsum('nkm,nkp->nmp', a, b, preferred_element_type=F32)


def _sigmoid(x):
    return jax.nn.sigmoid(x)


def _silu(x):
    return x * _sigmoid(x)


def _softplus(x):
    return jnp.maximum(x, 0.0) + jnp.log1p(jnp.exp(-jnp.abs(x)))


def _layer_norm(y, g, b):
    mu = jnp.mean(y, axis=-1, keepdims=True)
    yc = y - mu
    var = jnp.mean(yc * yc, axis=-1, keepdims=True)
    return yc * lax.rsqrt(var + LN_EPS) * g + b


def _chunk_cumsum(x, axis, reverse):
    n = x.shape[axis]
    pos = lax.broadcasted_iota(jnp.int32, x.shape, axis) % CHUNK
    s = 1
    while s < CHUNK:
        if reverse:
            shifted = pltpu.roll(x, n - s, axis)
            keep = pos < CHUNK - s
        else:
            shifted = pltpu.roll(x, s, axis)
            keep = pos >= s
        x = x + jnp.where(keep, shifted, 0.0)
        s *= 2
    return x


def _ffn_ln_kernel(x_ref, win_ref, wout_ref, g_ref, b_ref, o_ref, *, d_ff, n_split, sub):
    tiles = d_ff // MXU_DIM
    cuts = [MXU_DIM * (tiles * c // n_split) for c in range(n_split)] + [d_ff]
    for r0 in range(0, x_ref.shape[0], sub):
        x = x_ref[r0:r0 + sub, :]
        xb = x.astype(BF16)
        acc = jnp.zeros(x.shape, F32)
        for lo, hi in zip(cuts[:-1], cuts[1:]):
            gate = _dot(xb, win_ref[:, lo:hi])
            up = _dot(xb, win_ref[:, d_ff + lo:d_ff + hi])
            h = (_silu(gate) * up).astype(BF16)
            acc = acc + _dot(h, wout_ref[lo:hi, :])
        y = ALPHA * x + 0.5 * acc
        o_ref[r0:r0 + sub, :] = _layer_norm(y, g_ref[...], b_ref[...])


def _ffn_ln(x, w_in, w_out, g, b, *, tm):
    t, d = x.shape
    d_ff = w_out.shape[0]
    tm = min(tm, t)
    return pl.pallas_call(
        functools.partial(_ffn_ln_kernel, d_ff=d_ff, n_split=2, sub=min(FFN_SUB_ROWS, tm)),
        grid=(t // tm,),
        in_specs=[
            pl.BlockSpec((tm, d), lambda i: (i, 0)),
            _resident(w_in.shape),
            _resident(w_out.shape),
            _resident(g.shape),
            _resident(b.shape),
        ],
        out_specs=pl.BlockSpec((tm, d), lambda i: (i, 0)),
        out_shape=jax.ShapeDtypeStruct((t, d), F32),
        compiler_params=_cparams("parallel"),
        name="ffn_ln",
    )(x, w_in, w_out, g, b)


def _proj_kernel(x_ref, w_ref, o_ref):
    o_ref[...] = _dot(x_ref[...].astype(BF16), w_ref[...]).astype(o_ref.dtype)


def _proj(x, w, *, tm, out_dtype):
    t, d = x.shape
    n = w.shape[1]
    tm = min(tm, t)
    return pl.pallas_call(
        _proj_kernel,
        grid=(t // tm,),
        in_specs=[pl.BlockSpec((tm, d), lambda i: (i, 0)), _resident(w.shape)],
        out_specs=pl.BlockSpec((tm, n), lambda i: (i, 0)),
        out_shape=jax.ShapeDtypeStruct((t, n), out_dtype),
        compiler_params=_cparams("parallel"),
        name="mem_kv_proj",
    )(x, w)


def _xattn_ln_kernel(x_ref, kv_ref, wq_ref, wo_ref, g_ref, b_ref, o_ref, *, heads, sub):
    d = x_ref.shape[-1]
    dh = d // heads
    for r0 in range(0, x_ref.shape[1], sub):
        x = x_ref[0, r0:r0 + sub, :]
        q = _dot(x.astype(BF16), wq_ref[...]).astype(BF16)
        outs = []
        for h in range(heads):
            qh = q[:, h * dh:(h + 1) * dh]
            kh = kv_ref[0, :, h * dh:(h + 1) * dh]
            vh = kv_ref[0, :, d + h * dh:d + (h + 1) * dh]
            s = _dot_nt(qh, kh) * (dh ** -0.5)
            e = jnp.exp(s - jnp.max(s, axis=-1, keepdims=True))
            den = jnp.sum(e, axis=-1, keepdims=True)
            outs.append(_dot(e.astype(BF16), vh) / den)
        o = jnp.concatenate(outs, axis=-1).astype(BF16)
        y = ALPHA * x + _dot(o, wo_ref[...])
        o_ref[0, r0:r0 + sub, :] = _layer_norm(y, g_ref[...], b_ref[...])


def _xattn_ln(x, kv, w_q, w_o, g, b, *, tm):
    bsz, s, d = x.shape
    n_mem = kv.shape[1]
    tm = min(tm, s)
    return pl.pallas_call(
        functools.partial(_xattn_ln_kernel, heads=XA_HEADS, sub=min(SUB_ROWS, tm)),
        grid=(bsz, s // tm),
        in_specs=[
            pl.BlockSpec((1, tm, d), lambda bi, i: (bi, i, 0)),
            pl.BlockSpec((1, n_mem, 2 * d), lambda bi, i: (bi, 0, 0)),
            _resident(w_q.shape),
            _resident(w_o.shape),
            _resident(g.shape),
            _resident(b.shape),
        ],
        out_specs=pl.BlockSpec((1, tm, d), lambda bi, i: (bi, i, 0)),
        out_shape=jax.ShapeDtypeStruct((bsz, s, d), F32),
        compiler_params=_cparams("parallel", "parallel"),
        name="xattn_ln",
    )(x, kv, w_q, w_o, g, b)


def _mixer_out_kernel(of_ref, or_ref, r_ref, x_ref, nw_ref, wo_ref, g_ref, b_ref, o_ref, *, dv, head_major, sub):
    nw = nw_ref[...]
    for r0 in range(0, x_ref.shape[1], sub):
        rows = slice(r0, r0 + sub)
        parts = []
        for h in range(r_ref.shape[-1] // dv):
            if head_major:
                oh = of_ref[0, h, rows, :].astype(F32) + or_ref[0, h, rows, :].astype(F32)
            else:
                oh = (of_ref[0, rows, h * dv:(h + 1) * dv].astype(F32)
                      + or_ref[0, rows, h * dv:(h + 1) * dv].astype(F32))
            ms = jnp.mean(oh * oh, axis=-1, keepdims=True)
            parts.append(oh * lax.rsqrt(ms + NORM_EPS) * nw)
        gated = (jnp.concatenate(parts, axis=-1) * _silu(r_ref[0, rows, :].astype(F32))).astype(BF16)
        y = ALPHA * x_ref[0, rows, :] + _dot(gated, wo_ref[...])
        o_ref[0, rows, :] = _layer_norm(y, g_ref[...], b_ref[...])


def _mixer_out(o_f, o_r, r, x, norm_w, w_out, g, b, *, tm):
    bsz, s, d = x.shape
    dh = r.shape[-1]
    dv = norm_w.shape[-1]
    tm = min(tm, s)
    head_major = o_f.ndim == 4
    tok = lambda n: pl.BlockSpec((1, tm, n), lambda bi, i: (bi, i, 0))
    o_spec = pl.BlockSpec((1, dh // dv, tm, dv), lambda bi, i: (bi, 0, i, 0)) if head_major else tok(dh)
    return pl.pallas_call(
        functools.partial(_mixer_out_kernel, dv=dv, head_major=head_major, sub=min(SUB_ROWS, tm)),
        grid=(bsz, s // tm),
        in_specs=[
            o_spec, o_spec, tok(dh), tok(d),
            _resident(norm_w.shape),
            _resident(w_out.shape),
            _resident(g.shape),
            _resident(b.shape),
        ],
        out_specs=tok(d),
        out_shape=jax.ShapeDtypeStruct((bsz, s, d), F32),
        compiler_params=_cparams("parallel", "parallel"),
        name="mixer_out_ln",
    )(o_f, o_r, r, x, norm_w, w_out, g, b)


def _gla_proj_kernel(x_ref, win_ref, wgd_ref, wgu_ref, bg_ref,
                     q_ref, k_ref, v_ref, r_ref, g_ref, *, hk, hv, sub):
    dk = hk // GLA_HEADS
    for r0 in range(0, x_ref.shape[0], sub):
        rows = slice(r0, r0 + sub)
        xb = x_ref[rows, :].astype(BF16)
        q_ref[rows, :] = _dot(xb, win_ref[:, 0:hk]) * (dk ** -0.5)
        k_ref[rows, :] = _dot(xb, win_ref[:, hk:2 * hk])
        v_ref[rows, :] = _dot(xb, win_ref[:, 2 * hk:2 * hk + hv])
        r_ref[rows, :] = _dot(xb, win_ref[:, 2 * hk + hv:2 * hk + 2 * hv]).astype(r_ref.dtype)
        low = _dot(xb, wgd_ref[...]).astype(BF16)
        logit = _dot(low, wgu_ref[...]) + bg_ref[...]
        logsig = jnp.minimum(logit, 0.0) - jnp.log1p(jnp.exp(-jnp.abs(logit)))
        gate = logsig / GLA_GATE_NORMALIZER
        g_ref[0, rows, :] = gate[:, 0:hk]
        g_ref[1, rows, :] = gate[:, hk:2 * hk]


def _gla_proj(x, w_in, w_gd, w_gu, b_g, *, tm):
    t, d = x.shape
    hk = w_gu.shape[1] // 2
    hv = (w_in.shape[1] - 2 * hk) // 2
    tm = min(tm, t)
    row = lambda n: pl.BlockSpec((tm, n), lambda i: (i, 0))
    return pl.pallas_call(
        functools.partial(_gla_proj_kernel, hk=hk, hv=hv, sub=min(SUB_ROWS, tm)),
        grid=(t // tm,),
        in_specs=[row(d), _resident(w_in.shape), _resident(w_gd.shape),
                  _resident(w_gu.shape), _resident(b_g.shape)],
        out_specs=[row(hk), row(hk), row(hv), row(hv),
                   pl.BlockSpec((2, tm, hk), lambda i: (0, i, 0))],
        out_shape=[
            jax.ShapeDtypeStruct((t, hk), F32),
            jax.ShapeDtypeStruct((t, hk), F32),
            jax.ShapeDtypeStruct((t, hv), F32),
            jax.ShapeDtypeStruct((t, hv), BF16),
            jax.ShapeDtypeStruct((2, t, hk), F32),
        ],
        compiler_params=_cparams("parallel"),
        name="gla_proj",
    )(x, w_in, w_gd, w_gu, b_g)


def _gla_scan_kernel(q_ref, k_ref, v_ref, g_ref, o_ref, st_ref, *, reverse):
    @pl.when(pl.program_id(1) == 0)
    def _():
        st_ref[...] = jnp.zeros(st_ref.shape, F32)

    heads, dv, dk = st_ref.shape
    cb = q_ref.shape[1]
    nc = cb // CHUNK

    def by_head(a, width):
        return jnp.concatenate([a[:, h * width:(h + 1) * width].reshape(nc, CHUNK, width)
                                for h in range(heads)], axis=0)

    b = by_head(_chunk_cumsum(g_ref[0, 0], 0, reverse), dk)
    q = by_head(q_ref[0], dk)
    k = by_head(k_ref[0], dk)
    vb = by_head(v_ref[0].astype(BF16), dv)
    mid = CHUNK // 2 if reverse else CHUNK // 2 - 1
    last = 0 if reverse else CHUNK - 1
    b_mid = b[:, mid:mid + 1, :]
    b_last = b[:, last:last + 1, :]

    qe = (q * jnp.exp(b - b_mid)).astype(BF16)
    ke = (k * jnp.exp(b_mid - b)).astype(BF16)
    scores = _bmm_nt(qe, ke)
    row = lax.broadcasted_iota(jnp.int32, (CHUNK, CHUNK), 0)
    col = lax.broadcasted_iota(jnp.int32, (CHUNK, CHUNK), 1)
    visible = (col >= row) if reverse else (col <= row)
    o = _bmm(jnp.where(visible, scores, 0.0).astype(BF16), vb)

    q_start = (q * jnp.exp(b)).astype(BF16)
    k_end = (k * jnp.exp(b_last - b)).astype(BF16)
    decay = jnp.exp(b_last)
    zt = _bmm_tn(vb, k_end)
    states = [None] * (heads * nc)
    finals = []
    for h in range(heads):
        st = st_ref[h]
        for i in (range(nc - 1, -1, -1) if reverse else range(nc)):
            states[h * nc + i] = st.astype(BF16)
            st = st * decay[h * nc + i] + zt[h * nc + i]
        finals.append(st)
    o = o + _bmm_nt(q_start, jnp.stack(states))
    for h in range(heads):
        st_ref[h] = finals[h]
        o_ref[0, :, h * dv:(h + 1) * dv] = o[h * nc:(h + 1) * nc].reshape(cb, dv).astype(o_ref.dtype)


def _gla_scan(q, k, v, g, *, direction, cb):
    bsz, s, hk = q.shape
    hv = v.shape[-1]
    dk, dv = hk // GLA_HEADS, hv // GLA_HEADS
    cb = min(cb, s)
    nb = s // cb
    reverse = direction == 1
    blk = (lambda n: nb - 1 - n) if reverse else (lambda n: n)
    tok = lambda n: pl.BlockSpec((1, cb, n), lambda bi, n_: (bi, blk(n_), 0))
    return pl.pallas_call(
        functools.partial(_gla_scan_kernel, reverse=reverse),
        grid=(bsz, nb),
        in_specs=[tok(hk), tok(hk), tok(hv),
                  pl.BlockSpec((1, 1, cb, hk), lambda bi, n: (direction, bi, blk(n), 0))],
        out_specs=tok(hv),
        out_shape=jax.ShapeDtypeStruct((bsz, s, hv), BF16),
        scratch_shapes=[pltpu.VMEM((GLA_HEADS, dv, dk), F32)],
        compiler_params=_cparams("parallel", "arbitrary"),
        name="gla_scan_rev" if reverse else "gla_scan_fwd",
    )(q, k, v, g)


def _gdn_proj_kernel(x_ref, xp_ref, xn_ref, win_ref, cw_ref, wab_ref,
                     alog_ref, dtb_ref, alogt_ref, dtbt_ref,
                     q_ref, k_ref, v_ref, z_ref, gc_ref, beta_ref, gct_ref, p_ref, *, n_qk, n_v):
    i = pl.program_id(1)
    tm = x_ref.shape[1]
    xb = x_ref[0].astype(BF16)
    keep_prev = (i > 0).astype(F32)
    keep_next = (i < pl.num_programs(1) - 1).astype(F32)
    xx = jnp.concatenate([(xp_ref[0] * keep_prev).astype(BF16), xb,
                          (xn_ref[0] * keep_next).astype(BF16)], axis=0)
    n_conv = 2 * n_qk + n_v
    z_ref[0] = _dot(xb, win_ref[:, n_conv:n_conv + n_v]).astype(z_ref.dtype)

    hv = GDN_V_HEADS
    ab = _dot(xb, wab_ref[...])
    abt = ab.T
    for d in range(2):
        a = ab[:, 2 * d * hv:(2 * d + 1) * hv]
        bt = ab[:, (2 * d + 1) * hv:(2 * d + 2) * hv]
        g = -jnp.exp(alog_ref[d:d + 1, :]) * _softplus(a + dtb_ref[d:d + 1, :])
        gc_ref[d, 0] = _chunk_cumsum(g, 0, d == 1)
        beta_ref[d, 0] = _sigmoid(bt)
        at = abt[2 * d * hv:(2 * d + 1) * hv, :]
        gt = -jnp.exp(alogt_ref[d]) * _softplus(at + dtbt_ref[d])
        gct = _chunk_cumsum(gt, 1, d == 1)
        for c in range(tm // CHUNK):
            gct_ref[d, 0, c] = gct[:, c * CHUNK:(c + 1) * CHUNK]

    for j in range(n_conv // GDN_DK):
        cs = slice(j * GDN_DK, (j + 1) * GDN_DK)
        if j % 2 == 0:
            pg = _dot(xx, win_ref[:, j * GDN_DK:(j + 2) * GDN_DK])
            p_ref[j] = pg[:, :GDN_DK]
            p_ref[j + 1] = pg[:, GDN_DK:]
        y = jnp.zeros((tm, GDN_DK), F32)
        for tap in range(GDN_CONV):
            y = y + cw_ref[tap:tap + 1, cs] * p_ref[j, pl.ds(HALO - 2 + tap, tm), :]
        y = _silu(y)
        if j < 2 * GDN_QK_HEADS:
            y = y * lax.rsqrt(jnp.sum(y * y, axis=-1, keepdims=True) + NORM_EPS)
            if j < GDN_QK_HEADS:
                q_ref[0, j] = y * (GDN_DK ** -0.5)
            else:
                k_ref[0, j - GDN_QK_HEADS] = y
        else:
            v_ref[0, j - 2 * GDN_QK_HEADS] = y


def _gdn_proj(x, w_in, conv_w, w_ab, a_log, dt_bias, a_log_t, dt_bias_t, *, tm):
    bsz, s, d = x.shape
    hv = GDN_V_HEADS
    n_qk = GDN_QK_HEADS * GDN_DK
    n_v = hv * GDN_DK
    tm = min(tm, s)
    nt = s // tm
    hb = tm // HALO
    tok = lambda n: pl.BlockSpec((1, tm, n), lambda bi, i: (bi, i, 0))
    head = lambda n: pl.BlockSpec((1, n, tm, GDN_DK), lambda bi, i: (bi, 0, i, 0))
    gate = pl.BlockSpec((2, 1, tm, hv), lambda bi, i: (0, bi, i, 0))
    return pl.pallas_call(
        functools.partial(_gdn_proj_kernel, n_qk=n_qk, n_v=n_v),
        grid=(bsz, nt),
        in_specs=[
            tok(d),
            pl.BlockSpec((1, HALO, d), lambda bi, i: (bi, jnp.maximum(i * hb - 1, 0), 0)),
            pl.BlockSpec((1, HALO, d), lambda bi, i: (bi, jnp.minimum((i + 1) * hb, s // HALO - 1), 0)),
            _resident(w_in.shape), _resident(conv_w.shape), _resident(w_ab.shape),
            _resident(a_log.shape), _resident(dt_bias.shape),
            _resident(a_log_t.shape), _resident(dt_bias_t.shape),
        ],
        out_specs=[head(GDN_QK_HEADS), head(GDN_QK_HEADS), head(hv), tok(n_v), gate, gate,
                   pl.BlockSpec((2, 1, tm // CHUNK, hv, CHUNK), lambda bi, i: (0, bi, i, 0, 0))],
        out_shape=[
            jax.ShapeDtypeStruct((bsz, GDN_QK_HEADS, s, GDN_DK), F32),
            jax.ShapeDtypeStruct((bsz, GDN_QK_HEADS, s, GDN_DK), F32),
            jax.ShapeDtypeStruct((bsz, hv, s, GDN_DK), F32),
            jax.ShapeDtypeStruct((bsz, s, n_v), BF16),
            jax.ShapeDtypeStruct((2, bsz, s, hv), F32),
            jax.ShapeDtypeStruct((2, bsz, s, hv), F32),
            jax.ShapeDtypeStruct((2, bsz, s // CHUNK, hv, CHUNK), F32),
        ],
        scratch_shapes=[pltpu.VMEM(((2 * n_qk + n_v) // GDN_DK, tm + 2 * HALO, GDN_DK), F32)],
        compiler_params=_cparams("parallel", "parallel"),
        name="gdn_proj",
    )(x, x, x, w_in, conv_w, w_ab, a_log, dt_bias, a_log_t, dt_bias_t)


def _unit_tri_inverse(a):
    row = lax.broadcasted_iota(jnp.int32, (CHUNK, CHUNK), 0)
    col = lax.broadcasted_iota(jnp.int32, (CHUNK, CHUNK), 1)
    same = lambda s: (row // s) == (col // s)
    eye = (row == col).astype(F32)
    ab = a.astype(BF16)
    zero = jnp.zeros_like(ab)
    adb = jnp.where(same(8), ab, zero)
    ad = adb.astype(F32)
    a2 = _bmm(adb, adb)
    a2b = a2.astype(BF16)
    a4 = _bmm(a2b, a2b)
    p1 = eye - ad + a2 - _bmm(adb, a2b)
    t = p1 + _bmm(p1.astype(BF16), a4.astype(BF16))
    s = 8
    while s < CHUNK:
        off = jnp.logical_and(same(2 * s), jnp.logical_not(same(s)))
        tb = t.astype(BF16)
        x = _bmm(jnp.where(off, ab, zero), tb)
        t = t - _bmm(tb, x.astype(BF16))
        s *= 2
    return t


def _gdn_scan_kernel(q_ref, k_ref, v_ref, gc_ref, beta_ref, gct_ref, o_ref,
                     st_ref, u_ref, wq_ref, attn_ref, ke_ref, cdec_ref, *, reverse):
    @pl.when(pl.program_id(1) == 0)
    def _():
        st_ref[...] = jnp.zeros(st_ref.shape, F32)

    qk_heads, cb, dk = q_ref.shape[1:]
    hv = v_ref.shape[1]
    rep = hv // qk_heads
    nc = cb // CHUNK
    row = lax.broadcasted_iota(jnp.int32, (CHUNK, CHUNK), 0)
    col = lax.broadcasted_iota(jnp.int32, (CHUNK, CHUNK), 1)
    incl = (col >= row) if reverse else (col <= row)
    strict = (col > row) if reverse else (col < row)
    last = 0 if reverse else CHUNK - 1
    gn = SCAN_GROUP_CHUNKS
    groups = nc // gn

    def prep(g):
        chunks = slice(g * gn, (g + 1) * gn)
        rows = slice(g * gn * CHUNK, (g + 1) * gn * CHUNK)
        gc_all = gc_ref[0, 0, rows, :]
        beta_all = beta_ref[0, 0, rows, :]
        heads, a_list, rhs_list, late = [], [], [], []
        for p in range(qk_heads):
            q = q_ref[0, p, rows, :].reshape(gn, CHUNK, dk)
            k = k_ref[0, p, rows, :].reshape(gn, CHUNK, dk)
            kb = k.astype(BF16)
            kk = _bmm_nt(kb, kb)
            qk = _bmm_nt(q.astype(BF16), kb)
            for hh in range(rep):
                j = p * rep + hh
                gcol = gc_all[:, j:j + 1].reshape(gn, CHUNK, 1)
                bcol = beta_all[:, j:j + 1].reshape(gn, CHUNK, 1)
                grow = gct_ref[0, 0, chunks, j:j + 1, :]
                decay = jnp.where(incl, jnp.exp(jnp.where(incl, gcol - grow, 0.0)), 0.0)
                eg = jnp.exp(gcol)
                glast = gcol[:, last:last + 1, :]
                v = v_ref[0, j, rows, :].reshape(gn, CHUNK, dk)
                heads.append(j)
                a_list.append(jnp.where(strict, bcol * kk * decay, 0.0))
                rhs_list.append(jnp.concatenate([v * bcol, k * (bcol * eg)], axis=-1).astype(BF16))
                late.append(((q * eg).astype(BF16), (qk * decay).astype(BF16),
                             (k * jnp.exp(glast - gcol)).astype(BF16),
                             jnp.broadcast_to(jnp.exp(glast), (gn, 1, dk))))
        t = _unit_tri_inverse(jnp.concatenate(a_list, axis=0)).astype(BF16)
        uw = _bmm(t, jnp.concatenate(rhs_list, axis=0))
        for n, (j, (q_start, attn, k_end, cdec)) in enumerate(zip(heads, late)):
            uw_j = uw[n * gn:(n + 1) * gn]
            u_ref[chunks, j] = uw_j[:, :, :dk]
            wq_ref[chunks, j] = jnp.concatenate([uw_j[:, :, dk:].astype(BF16), q_start], axis=1)
            attn_ref[chunks, j] = attn
            ke_ref[chunks, j] = k_end
            cdec_ref[chunks, j] = cdec

    def scan(g):
        for c in (range(gn - 1, -1, -1) if reverse else range(gn)):
            i = g * gn + c
            st = st_ref[...]
            ws = _bmm(wq_ref[i], st.astype(BF16))
            v_new = (u_ref[i] - ws[:, :CHUNK]).astype(BF16)
            o_ref[0, :, i * CHUNK:(i + 1) * CHUNK, :] = (ws[:, CHUNK:] + _bmm(attn_ref[i], v_new)).astype(o_ref.dtype)
            st_ref[...] = st * cdec_ref[i] + _bmm_tn(ke_ref[i], v_new)

    order = list(range(groups - 1, -1, -1) if reverse else range(groups))
    prep(order[0])
    for prev, cur in zip(order[:-1], order[1:]):
        prep(cur)
        scan(prev)
    scan(order[-1])


def _gdn_scan(q, k, v, gc, beta, gct, *, direction, cb):
    bsz, qk_heads, s, dk = q.shape
    hv = v.shape[1]
    cb = min(cb, s)
    nb = s // cb
    nc = cb // CHUNK
    reverse = direction == 1
    blk = (lambda n: nb - 1 - n) if reverse else (lambda n: n)
    head_major = lambda h: pl.BlockSpec((1, h, cb, dk), lambda bi, n: (bi, 0, blk(n), 0))
    gate = pl.BlockSpec((1, 1, cb, hv), lambda bi, n: (direction, bi, blk(n), 0))
    return pl.pallas_call(
        functools.partial(_gdn_scan_kernel, reverse=reverse),
        grid=(bsz, nb),
        in_specs=[
            head_major(qk_heads), head_major(qk_heads), head_major(hv), gate, gate,
            pl.BlockSpec((1, 1, nc, hv, CHUNK), lambda bi, n: (direction, bi, blk(n), 0, 0)),
        ],
        out_specs=head_major(hv),
        out_shape=jax.ShapeDtypeStruct((bsz, hv, s, dk), BF16),
        scratch_shapes=[
            pltpu.VMEM((hv, dk, dk), F32),
            pltpu.VMEM((nc, hv, CHUNK, dk), F32),
            pltpu.VMEM((nc, hv, 2 * CHUNK, dk), BF16),
            pltpu.VMEM((nc, hv, CHUNK, CHUNK), BF16),
            pltpu.VMEM((nc, hv, CHUNK, dk), BF16),
            pltpu.VMEM((nc, hv, 1, dk), F32),
        ],
        compiler_params=_cparams("parallel", "arbitrary"),
        name="gdn_scan_rev" if reverse else "gdn_scan_fwd",
    )(q, k, v, gc, beta, gct)


def _prepare_weights(ffn_w_in, ffn_w_out, ln_g, ln_b,
                     gla_w_in, gla_w_gate_down, gla_w_gate_up, gla_b_gate, gla_norm_w, gla_w_out,
                     gdn_w_in, gdn_conv_w, gdn_w_ab, gdn_a_log, gdn_dt_bias, gdn_norm_w, gdn_w_out,
                     xa_w_q, xa_w_kv, xa_w_o):
    n_gla, _, d, rank = gla_w_gate_down.shape
    hk = gla_w_gate_up.shape[-1]
    w_gd = jnp.transpose(gla_w_gate_down, (0, 2, 1, 3)).reshape(n_gla, d, 2 * rank)
    zeros = jnp.zeros((n_gla, rank, hk), F32)
    w_gu = jnp.concatenate([
        jnp.concatenate([gla_w_gate_up[:, 0], zeros], axis=-1),
        jnp.concatenate([zeros, gla_w_gate_up[:, 1]], axis=-1)], axis=1)
    n_gdn = gdn_w_ab.shape[0]
    w_ab = jnp.transpose(gdn_w_ab, (0, 2, 1, 3)).reshape(n_gdn, d, -1)
    return dict(
        ffn_w_in=ffn_w_in.astype(BF16), ffn_w_out=ffn_w_out.astype(BF16),
        ln_g=ln_g[:, :, None, :], ln_b=ln_b[:, :, None, :],
        gla_w_in=gla_w_in.astype(BF16), gla_w_gd=w_gd.astype(BF16), gla_w_gu=w_gu.astype(BF16),
        gla_b_g=gla_b_gate.reshape(n_gla, 1, 2 * hk), gla_norm_w=gla_norm_w[:, None, :],
        gla_w_out=gla_w_out.astype(BF16),
        gdn_w_in=gdn_w_in.astype(BF16), gdn_conv_w=gdn_conv_w,
        gdn_w_ab=jnp.pad(w_ab, ((0, 0), (0, 0), (0, LANES - w_ab.shape[-1]))).astype(BF16),
        gdn_a_log=gdn_a_log, gdn_dt_bias=gdn_dt_bias,
        gdn_a_log_t=gdn_a_log[..., None], gdn_dt_bias_t=gdn_dt_bias[..., None],
        gdn_norm_w=gdn_norm_w[:, None, :], gdn_w_out=gdn_w_out.astype(BF16),
        xa_w_q=xa_w_q.astype(BF16), xa_w_kv=xa_w_kv.astype(BF16), xa_w_o=xa_w_o.astype(BF16),
    )


def _trunk(x, mem, w):
    bsz, s, d = x.shape
    t = bsz * s
    tm = 512
    tm_wide = 2 * SUB_ROWS
    cb = 512
    for i in range(DEPTH):
        ln = lambda n: (w['ln_g'][i, n], w['ln_b'][i, n])
        x2 = _ffn_ln(x.reshape(t, d), w['ffn_w_in'][i, 0], w['ffn_w_out'][i, 0], *ln(0), tm=tm_wide)
        j = i // N_MIXERS
        if i % N_MIXERS == 0:
            q, k, v, r, g = _gla_proj(x2, w['gla_w_in'][j], w['gla_w_gd'][j], w['gla_w_gu'][j],
                                      w['gla_b_g'][j], tm=tm_wide)
            q, k, v, r = (a.reshape(bsz, s, -1) for a in (q, k, v, r))
            g = g.reshape(2, bsz, s, -1)
            o_f = _gla_scan(q, k, v, g, direction=0, cb=2 * cb)
            o_r = _gla_scan(q, k, v, g, direction=1, cb=2 * cb)
            norm_w, w_out = w['gla_norm_w'][j], w['gla_w_out'][j]
        else:
            q, k, v, r, gc, beta, gct = _gdn_proj(
                x2.reshape(bsz, s, d), w['gdn_w_in'][j], w['gdn_conv_w'][j], w['gdn_w_ab'][j],
                w['gdn_a_log'][j], w['gdn_dt_bias'][j],
                w['gdn_a_log_t'][j], w['gdn_dt_bias_t'][j], tm=tm)
            o_f = _gdn_scan(q, k, v, gc, beta, gct, direction=0, cb=cb)
            o_r = _gdn_scan(q, k, v, gc, beta, gct, direction=1, cb=cb)
            norm_w, w_out = w['gdn_norm_w'][j], w['gdn_w_out'][j]
        x2 = _mixer_out(o_f, o_r, r, x2.reshape(bsz, s, d), norm_w, w_out, *ln(1), tm=tm_wide)
        kv = _proj(mem.reshape(-1, d), w['xa_w_kv'][i], tm=256, out_dtype=BF16)
        x3 = _xattn_ln(x2, kv.reshape(bsz, -1, 2 * d),
                       w['xa_w_q'][i], w['xa_w_o'][i], *ln(2), tm=tm_wide)
        x = _ffn_ln(x3.reshape(t, d), w['ffn_w_in'][i, 1], w['ffn_w_out'][i, 1], *ln(3), tm=tm_wide)
        x = x.reshape(bsz, s, d)
    return x


def kernel(x_prompt, x_sample, mem_prompt, mem_sample, ffn_w_in, ffn_w_out, ln_g, ln_b, gla_w_in, gla_w_gate_down, gla_w_gate_up, gla_b_gate, gla_norm_w, gla_w_out, gdn_w_in, gdn_conv_w, gdn_w_ab, gdn_a_log, gdn_dt_bias, gdn_norm_w, gdn_w_out, xa_w_q, xa_w_kv, xa_w_o):
    w = _prepare_weights(ffn_w_in, ffn_w_out, ln_g, ln_b,
                         gla_w_in, gla_w_gate_down, gla_w_gate_up, gla_b_gate, gla_norm_w, gla_w_out,
                         gdn_w_in, gdn_conv_w, gdn_w_ab, gdn_a_log, gdn_dt_bias, gdn_norm_w, gdn_w_out,
                         xa_w_q, xa_w_kv, xa_w_o)
    return (_trunk(x_prompt, mem_prompt, w), _trunk(x_sample, mem_sample, w))
```
